```python
import jax, jax.numpy as jnp
from jax import lax
import numpy as np

D_MODEL = 2048
BATCH = 4
SEQ = 2048
DEPTH = 2

POOL_WIDTH = D_MODEL // 4
POOL_WINDOWS = (2, 4, 8, 16)
POOL_GROUPS = len(POOL_WINDOWS)
POOL_GROUP_DIM = POOL_WIDTH // POOL_GROUPS
SB_HEAD_DIM = 128
SB_WIDTH = D_MODEL // 2
SB_HEADS = SB_WIDTH // SB_HEAD_DIM
Q_BLOCK = 128
HG_HEAD_DIM = 128
HG_WIDTH = D_MODEL // 4
HG_HEADS = HG_WIDTH // HG_HEAD_DIM
HG_CHUNK = 64
LB_FLOOR = 1e-20
N_BRANCH = 3
D_FF = -(-8 * D_MODEL // (3 * 256)) * 256
PLE_DIM = 256
EPS = 1e-6

IN_SPLITS = (POOL_WIDTH, SB_WIDTH, SB_WIDTH, SB_WIDTH,
             HG_WIDTH, HG_WIDTH, HG_WIDTH, HG_WIDTH, N_BRANCH * D_MODEL)
IN_COLS = sum(IN_SPLITS)

kernel_name = "pool_stickbreak_hgrn2_gated_hybrid"


def rms_norm(x, g):
    xf = x.astype(jnp.float32)
    y = xf * lax.rsqrt(jnp.mean(xf * xf, axis=-1, keepdims=True) + EPS)
    return (y * g.astype(jnp.float32)).astype(x.dtype)


def pool_mixer(u, w_group, scale):
    B, S, _ = u.shape
    ug = u.reshape(B, S, POOL_GROUPS, POOL_GROUP_DIM).astype(jnp.float32)
    c = jnp.cumsum(ug, axis=1)
    pos = jnp.arange(S)
    means = []
    for gi, w in enumerate(POOL_WINDOWS):
        cp = jnp.pad(c[:, :, gi], ((0, 0), (w, 0), (0, 0)))
        win_sum = cp[:, w:w + S] - cp[:, :S]
        cnt = jnp.minimum(pos + 1, w).astype(jnp.float32)
        means.append(win_sum / cnt[None, :, None])
    mean = jnp.stack(means, axis=2)
    mixed = (mean - ug).astype(u.dtype)
    y = jnp.einsum('bsgc,gcd->bsgd', mixed, w_group)
    return y.reshape(B, S, POOL_WIDTH) * scale


def stick_breaking_attention(q, k, v):
    S = q.shape[2]
    scale = SB_HEAD_DIM ** -0.5
    outs = []
    for blk in range(S // Q_BLOCK):
        q0 = blk * Q_BLOCK
        kend = q0 + Q_BLOCK
        qb = q[:, :, q0:kend]
        kb = k[:, :, :kend]
        vb = v[:, :, :kend]
        z = jnp.einsum('bhtd,bhsd->bhts', qb, kb).astype(jnp.float32) * scale
        tpos = q0 + jnp.arange(Q_BLOCK)
        spos = jnp.arange(kend)
        mask = spos[None, :] < tpos[:, None]
        log_1mb = jnp.where(mask, jax.nn.log_sigmoid(-z), 0.0)
        later = lax.cumsum(log_1mb, axis=3, reverse=True) - log_1mb
        a = jnp.where(mask, jnp.exp(jax.nn.log_sigmoid(z) + later), 0.0)
        outs.append(jnp.einsum('bhts,bhsd->bhtd', a.astype(v.dtype), vb))
    return jnp.concatenate(outs, axis=2)


def hgrn2_recurrence(q, k, v, log_f):
    B, H, S, Dk = q.shape
    Dv = v.shape[-1]
    n = S // HG_CHUNK

    def to_chunks(a):
        return a.reshape(B, H, n, HG_CHUNK, a.shape[-1]).transpose(2, 0, 1, 3, 4)

    qc, kc, vc, fc = to_chunks(q), to_chunks(k), to_chunks(v), to_chunks(log_f)
    causal = jnp.tril(jnp.ones((HG_CHUNK, HG_CHUNK), dtype=bool))[:, :, None]

    def step(state, inp):
        qi, ki, vi, fi = inp
        b = jnp.cumsum(fi, axis=2)
        o_inter = jnp.einsum('bhtk,bhkv->bhtv', qi * jnp.exp(b), state)
        diff = b[:, :, :, None, :] - b[:, :, None, :, :]
        decay = jnp.where(causal, jnp.exp(jnp.minimum(diff, 0.0)), 0.0)
        scores = jnp.einsum('bhtk,bhsk,bhtsk->bhts', qi, ki, decay)
        o_intra = jnp.einsum('bhts,bhsv->bhtv', scores, vi)
        b_last = b[:, :, -1:, :]
        k_dec = ki * jnp.exp(b_last - b)
        new_state = state * jnp.exp(b_last[:, :, 0, :, None]) + jnp.einsum('bhsk,bhsv->bhkv', k_dec, vi)
        return new_state, o_inter + o_intra

    state0 = jnp.zeros((B, H, Dk, Dv), jnp.float32)
    _, ys = lax.scan(step, state0, (qc, kc, vc, fc))
    return ys.transpose(1, 2, 0, 3, 4).reshape(B, H, S, Dv)


def setup_inputs(seed: int = 0) -> dict:
    key = jax.random.key(seed)
    ks = jax.random.split(key, 20)
    f32 = jnp.float32

    def dense(k, shape, fan_in):
        return jax.random.normal(k, shape, f32) * (fan_in ** -0.5)

    def gain(k, shape):
        return 1.0 + 0.02 * jax.random.normal(k, shape, f32)

    return {
        "x": jax.random.normal(ks[0], (BATCH, SEQ, D_MODEL), f32),
        "p": jax.random.normal(ks[1], (DEPTH, BATCH, SEQ, PLE_DIM), f32),
        "norm_mix": gain(ks[2], (DEPTH, D_MODEL)),
        "w_in": dense(ks[3], (DEPTH, D_MODEL, IN_COLS), D_MODEL),
        "pool_w": dense(ks[4], (DEPTH, POOL_GROUPS, POOL_GROUP_DIM, POOL_GROUP_DIM), POOL_GROUP_DIM),
        "pool_scale": gain(ks[5], (DEPTH, POOL_WIDTH)),
        "hg_lb": 0.1 * jax.random.normal(ks[6], (DEPTH, HG_WIDTH), f32),
        "hg_norm": gain(ks[7], (DEPTH, HG_WIDTH)),
        "w_br_pool": dense(ks[8], (DEPTH, POOL_WIDTH, D_MODEL), POOL_WIDTH),
        "w_br_sb": dense(ks[9], (DEPTH, SB_WIDTH, D_MODEL), SB_WIDTH),
        "w_br_hg": dense(ks[10], (DEPTH, HG_WIDTH, D_MODEL), HG_WIDTH),
        "w_out": dense(ks[11], (DEPTH, D_MODEL, D_MODEL), D_MODEL),
        "norm_ffn": gain(ks[12], (DEPTH, D_MODEL)),
        "w_gate_up": dense(ks[13], (DEPTH, D_MODEL, 2 * D_FF), D_MODEL),
        "w_down": dense(ks[14], (DEPTH, D_FF, D_MODEL), D_FF),
        "norm_ple": gain(ks[15], (DEPTH, D_MODEL)),
        "w_ple_gate": dense(ks[16], (DEPTH, D_MODEL, D_MODEL), D_MODEL),
        "w_ple_proj": dense(ks[17], (DEPTH, PLE_DIM, D_MODEL), PLE_DIM),
        "norm_final": gain(ks[18], (D_MODEL,)),
    }


def reference(x, p, norm_mix, w_in, pool_w, pool_scale, hg_lb, hg_norm, w_br_pool, w_br_sb, w_br_hg,
              w_out, norm_ffn, w_gate_up, w_down, norm_ple, w_ple_gate, w_ple_proj, norm_final):
    B, S, _ = x.shape
    split_points = [int(c) for c in np.cumsum(IN_SPLITS)[:-1]]

    def heads(t, n):
        return t.reshape(B, S, n, -1).transpose(0, 2, 1, 3)

    def merge(t):
        return t.transpose(0, 2, 1, 3).reshape(B, S, -1)

    lb_sm = jax.nn.softmax(hg_lb.astype(jnp.float32), axis=0)
    lower_bounds = jnp.cumsum(lb_sm, axis=0) - lb_sm[0:1]

    for i in range(DEPTH):
        h = rms_norm(x, norm_mix[i])
        proj = h @ w_in[i]
        u_pool, sq, sk, sv, zf, hv, hq, og, gl = jnp.split(proj, split_points, axis=-1)

        y_pool = pool_mixer(u_pool, pool_w[i], pool_scale[i])

        y_sb = merge(stick_breaking_attention(heads(sq, SB_HEADS), heads(sk, SB_HEADS), heads(sv, SB_HEADS)))

        lb = jnp.clip(lower_bounds[i], 0.0, 1.0)
        zf32 = zf.astype(jnp.float32)
        log_f = jnp.logaddexp(jnp.log(jnp.maximum(lb, LB_FLOOR)),
                              jnp.log1p(-jnp.minimum(lb, 1.0 - 1e-6)) + jax.nn.log_sigmoid(zf32))
        k_in = (1.0 - lb) * jax.nn.sigmoid(-zf32)
        q_hg = jax.nn.silu(hq.astype(jnp.float32))
        o = hgrn2_recurrence(heads(q_hg, HG_HEADS), heads(k_in, HG_HEADS),
                             heads(hv.astype(jnp.float32), HG_HEADS), heads(log_f, HG_HEADS))
        o = o * lax.rsqrt(jnp.mean(o * o, axis=-1, keepdims=True) + EPS)
        y_hg = (merge(o) * hg_norm[i].astype(jnp.float32) * jax.nn.silu(og.astype(jnp.float32))).astype(x.dtype)

        gates = jax.nn.sigmoid(gl).reshape(B, S, N_BRANCH, D_MODEL)
        mixed = (gates[:, :, 0] * (y_pool @ w_br_pool[i])
                 + gates[:, :, 1] * (y_sb @ w_br_sb[i])
                 + gates[:, :, 2] * (y_hg @ w_br_hg[i]))
        x = x + mixed @ w_out[i]

        h = rms_norm(x, norm_ffn[i])
        g_ff, u_ff = jnp.split(h @ w_gate_up[i], 2, axis=-1)
        x = x + (jax.nn.silu(g_ff) * u_ff) @ w_down[i]

        ple_gate = jax.nn.sigmoid(rms_norm(x, norm_ple[i]) @ w_ple_gate[i])
        x = x + ple_gate * (p[i] @ w_ple_proj[i])

    return rms_norm(x, norm_final)
```

```python
import functools

import jax
import jax.numpy as jnp
from jax import lax
from jax.experimental import pallas as pl
from jax.experimental.pallas import tpu as pltpu

D_MODEL = 2048
POOL_WIDTH = 512
POOL_WINDOWS = (2, 4, 8, 16)
POOL_GROUP_DIM = 128
SB_WIDTH = 1024
SB_HEAD_DIM = 128
SB_HEADS = 8
HG_WIDTH = 512
HG_HEAD_DIM = 128
HG_HEADS = 4
LB_FLOOR = 1e-20
D_FF = 5632
PLE_DIM = 256
EPS = 1e-6
IN_COLS = POOL_WIDTH + 3 * SB_WIDTH + 4 * HG_WIDTH + 3 * D_MODEL

LANE = 128

COL_POOL = 0
COL_SQ = POOL_WIDTH // LANE
COL_SK = COL_SQ + SB_WIDTH // LANE
COL_SV = COL_SK + SB_WIDTH // LANE
COL_ZF = COL_SV + SB_WIDTH // LANE
COL_HV = COL_ZF + HG_WIDTH // LANE
COL_HQ = COL_HV + HG_WIDTH // LANE
COL_OG = COL_HQ + HG_WIDTH // LANE
COL_GL = COL_OG + HG_WIDTH // LANE

ATT_BLOCK = 128
HG_CHUNK = 128
HG_BAND = 16
VMEM_LIMIT = 52 * 1024 * 1024

_F32 = jnp.float32
_BF16 = jnp.bfloat16


def _params(*sem):
    return pltpu.CompilerParams(dimension_semantics=sem, vmem_limit_bytes=VMEM_LIMIT)


def _dot(a, b):
    return jnp.dot(a, b, preferred_element_type=_F32)


def _dot_nt(a, b):
    return lax.dot_general(a, b, (((1,), (1,)), ((), ())), preferred_element_type=_F32)


def _dot_tn(a, b):
    return lax.dot_general(a, b, (((0,), (0,)), ((), ())), preferred_element_type=_F32)


def _split_dot(lhs_fn, x):
    hi = x.astype(_BF16)
    lo = (x - hi.astype(_F32)).astype(_BF16)
    return lhs_fn(hi) + lhs_fn(lo)


def _rms_rows(x, g):
    return x * lax.rsqrt(jnp.mean(x * x, axis=-1, keepdims=True) + EPS) * g


def _silu(x):
    return x * jax.nn.sigmoid(x)


def _norm_matmul_kernel(x_ref, g_ref, w_ref, o_ref, hn_ref):
    @pl.when(pl.program_id(1) == 0)
    def _():
        hn_ref[...] = _rms_rows(x_ref[...], g_ref[...]).astype(_BF16)

    o_ref[...] = _dot(hn_ref[...], w_ref[...])


def norm_matmul(x, g, w, tm=1024, tn=512):
    T, D = x.shape
    N = w.shape[1]
    return pl.pallas_call(
        _norm_matmul_kernel,
        grid=(T // tm, N // tn),
        in_specs=[pl.BlockSpec((tm, D), lambda i, j: (i, 0)),
                  pl.BlockSpec((1, D), lambda i, j: (0, 0)),
                  pl.BlockSpec((D, tn), lambda i, j: (0, j))],
        out_specs=pl.BlockSpec((tm, tn), lambda i, j: (i, j)),
        out_shape=jax.ShapeDtypeStruct((T, N), _F32),
        scratch_shapes=[pltpu.VMEM((tm, D), _BF16)],
        compiler_params=_params("arbitrary", "arbitrary"),
        name="norm_in_proj",
    )(x, g.reshape(1, D), w)


def _pool_kernel(u_ref, w_ref, s_ref, o_ref):
    S = u_ref.shape[0]
    t = lax.broadcasted_iota(jnp.int32, (S, POOL_GROUP_DIM), 0)
    for gi, win in enumerate(POOL_WINDOWS):
        cols = slice(gi * POOL_GROUP_DIM, (gi + 1) * POOL_GROUP_DIM)
        u = u_ref[:, cols]
        acc = u
        shift = 1
        while shift < win:
            acc = acc + jnp.where(t >= shift, pltpu.roll(acc, shift, axis=0), 0.0)
            shift *= 2
        cnt = jnp.minimum(t + 1, win).astype(_F32)
        mixed = (acc / cnt - u).astype(_BF16)
        y = _dot(mixed, w_ref[gi]) * s_ref[:, cols]
        o_ref[:, cols] = y.astype(o_ref.dtype)


def pool_mixer(proj3, pool_w, pool_scale):
    B, S, _ = proj3.shape
    return pl.pallas_call(
        _pool_kernel,
        grid=(B,),
        in_specs=[pl.BlockSpec((None, S, POOL_WIDTH), lambda b: (b, 0, COL_POOL)),
                  pl.BlockSpec((len(POOL_WINDOWS), POOL_GROUP_DIM, POOL_GROUP_DIM), lambda b: (0, 0, 0)),
                  pl.BlockSpec((1, POOL_WIDTH), lambda b: (0, 0))],
        out_specs=pl.BlockSpec((None, S, POOL_WIDTH), lambda b: (b, 0, 0)),
        out_shape=jax.ShapeDtypeStruct((B, S, POOL_WIDTH), _BF16),
        compiler_params=_params("arbitrary"),
        name="pool_mixer",
    )(proj3, pool_w, pool_scale.reshape(1, POOL_WIDTH))


def _sb_kernel(q_ref, k_ref, v_ref, o_ref):
    i = pl.program_id(2)
    blk = ATT_BLOCK
    scale = SB_HEAD_DIM ** -0.5
    q = q_ref[...].astype(_BF16)
    row = lax.broadcasted_iota(jnp.int32, (blk, blk), 0)
    col = lax.broadcasted_iota(jnp.int32, (blk, blk), 1)
    r2 = lax.broadcasted_iota(jnp.int32, (blk, 2 * blk), 0)
    c2 = lax.broadcasted_iota(jnp.int32, (blk, 2 * blk), 1)
    suffix = jnp.where((r2 > c2) | (c2 >= blk), 1.0, 0.0).astype(_BF16)

    def body(step, carry):
        acc, later_blocks = carry
        j = i - step
        start = pl.multiple_of(j * blk, blk)
        kb = k_ref[pl.ds(start, blk), :].astype(_BF16)
        vb = v_ref[pl.ds(start, blk), :].astype(_BF16)
        z = _dot_nt(q, kb) * scale
        mask = (col + j * blk) < (row + i * blk)
        log_1mb = -(jnp.maximum(z, 0.0) + jnp.log1p(jnp.exp(-jnp.abs(z))))
        masked = jnp.where(mask, log_1mb, 0.0)
        sums = _split_dot(lambda part: _dot(part, suffix), masked)
        later = sums[:, :blk] + later_blocks
        a = jnp.where(mask, jnp.exp(z + log_1mb + later), 0.0)
        acc = acc + _dot(a.astype(_BF16), vb)
        return acc, later_blocks + sums[:, blk:]

    zero = jnp.zeros((blk, blk), _F32)
    acc, _ = lax.fori_loop(0, i + 1, body, (zero, zero))
    o_ref[...] = acc.astype(o_ref.dtype)


def stick_breaking(proj3):
    B, S, _ = proj3.shape
    blk = ATT_BLOCK
    return pl.pallas_call(
        _sb_kernel,
        grid=(B, SB_HEADS, S // blk),
        in_specs=[pl.BlockSpec((None, blk, SB_HEAD_DIM), lambda b, h, i: (b, i, COL_SQ + h)),
                  pl.BlockSpec((None, S, SB_HEAD_DIM), lambda b, h, i: (b, 0, COL_SK + h)),
                  pl.BlockSpec((None, S, SB_HEAD_DIM), lambda b, h, i: (b, 0, COL_SV + h))],
        out_specs=pl.BlockSpec((None, blk, SB_HEAD_DIM), lambda b, h, i: (b, i, h)),
        out_shape=jax.ShapeDtypeStruct((B, S, SB_WIDTH), _BF16),
        compiler_params=_params("arbitrary", "arbitrary", "arbitrary"),
        name="stick_breaking",
    )(proj3, proj3, proj3)


def _hg_kernel(layer, zf_ref, hv_ref, hq_ref, og_ref, lb_ref, gn_ref, o_ref,
               state_ref, kpad_ref, bpad_ref, vpad_ref):
    C, W, pad = HG_CHUNK, HG_HEAD_DIM, HG_BAND

    @pl.when(pl.program_id(2) == 0)
    def _():
        state_ref[...] = jnp.zeros_like(state_ref)

    lbs = [lb_ref[d] for d in range(lb_ref.shape[0])]
    top = functools.reduce(jnp.maximum, lbs)
    es = [jnp.exp(row - top) for row in lbs]
    total = functools.reduce(jnp.add, es)
    sm = [e / total for e in es]
    lb = jnp.clip(functools.reduce(jnp.add, sm[:layer + 1]) - sm[0], 0.0, 1.0)

    z = zf_ref[...]
    abs_z = jnp.log1p(jnp.exp(-jnp.abs(z)))
    log_sig = jnp.minimum(z, 0.0) - abs_z
    la = jnp.log(jnp.maximum(lb, LB_FLOOR))
    lc = jnp.log1p(-jnp.minimum(lb, 1.0 - 1e-6)) + log_sig
    log_f = jnp.maximum(la, lc) + jnp.log1p(jnp.exp(-jnp.abs(la - lc)))
    k = (1.0 - lb) * jax.nn.sigmoid(-z)
    q = _silu(hq_ref[...])
    v = hv_ref[...]

    r = lax.broadcasted_iota(jnp.int32, (C, C), 0)
    c = lax.broadcasted_iota(jnp.int32, (C, C), 1)
    tri = jnp.where(c <= r, 1.0, 0.0).astype(_BF16)
    b = _split_dot(lambda part: _dot(tri, part), log_f)

    state = state_ref[...]
    o = _dot_nt((q * jnp.exp(b)).astype(_BF16), state.astype(_BF16))

    scores = jnp.zeros((C, C), _F32)
    m = C // 2
    while m >= HG_BAND:
        span = 2 * m
        ref = jnp.concatenate(
            [jnp.broadcast_to(b[g * span + m - 1:g * span + m, :], (span, W)) for g in range(C // span)], axis=0)
        qt = q * jnp.exp(jnp.minimum(b - ref, 0.0))
        kt = k * jnp.exp(jnp.minimum(ref - b, 0.0))
        s_m = _dot_nt(qt.astype(_BF16), kt.astype(_BF16))
        sel = ((r // span) == (c // span)) & ((r % span) >= m) & ((c % span) < m)
        scores = jnp.where(sel, s_m, scores)
        m //= 2
    o = o + _dot(scores.astype(_BF16), v.astype(_BF16))

    zeros_pad = jnp.zeros((pad, W), _F32)
    for ref_, val in ((kpad_ref, k), (bpad_ref, b), (vpad_ref, v)):
        ref_[0:pad, :] = zeros_pad
        ref_[pad:pad + C, :] = val
    sub = lax.broadcasted_iota(jnp.int32, (C, 1), 0) % HG_BAND
    for d in range(HG_BAND):
        ks = kpad_ref[pad - d:pad - d + C, :]
        bs = bpad_ref[pad - d:pad - d + C, :]
        vs = vpad_ref[pad - d:pad - d + C, :]
        w = jnp.sum(q * ks * jnp.exp(jnp.minimum(b - bs, 0.0)), axis=-1, keepdims=True)
        o = o + jnp.where(sub >= d, w, 0.0) * vs

    b_last = b[C - 1:C, :]
    k_dec = k * jnp.exp(b_last - b)
    state_ref[...] = state * jnp.exp(b_last) + _dot_tn(v.astype(_BF16), k_dec.astype(_BF16))

    o = o * lax.rsqrt(jnp.mean(o * o, axis=-1, keepdims=True) + EPS)
    o_ref[...] = (o * gn_ref[...] * _silu(og_ref[...])).astype(o_ref.dtype)


def hgrn2_mixer(proj3, hg_lb, hg_norm_l, layer):
    B, S, _ = proj3.shape
    depth = hg_lb.shape[0]
    C, W = HG_CHUNK, HG_HEAD_DIM

    def col(base):
        return pl.BlockSpec((None, C, W), lambda b, h, c: (b, c, base + h))

    return pl.pallas_call(
        functools.partial(_hg_kernel, layer),
        grid=(B, HG_HEADS, S // C),
        in_specs=[col(COL_ZF), col(COL_HV), col(COL_HQ), col(COL_OG),
                  pl.BlockSpec((depth, None, 1, W), lambda b, h, c: (0, h, 0, 0)),
                  pl.BlockSpec((None, 1, W), lambda b, h, c: (h, 0, 0))],
        out_specs=pl.BlockSpec((None, C, W), lambda b, h, c: (b, c, h)),
        out_shape=jax.ShapeDtypeStruct((B, S, HG_WIDTH), _BF16),
        scratch_shapes=[pltpu.VMEM((W, W), _F32)] + [pltpu.VMEM((HG_BAND + C, W), _F32)] * 3,
        compiler_params=_params("arbitrary", "arbitrary", "arbitrary"),
        name="hgrn2_mixer",
    )(proj3, proj3, proj3, proj3,
      hg_lb.reshape(depth, HG_HEADS, 1, W), hg_norm_l.reshape(HG_HEADS, 1, W))


def _merge_kernel(x_ref, yp_ref, ys_ref, yh_ref, g0_ref, g1_ref, g2_ref,
                  wp_ref, ws_ref, wh_ref, wo_ref, o_ref, acc_ref):
    n = pl.program_id(1)

    @pl.when(n == 0)
    def _():
        acc_ref[...] = jnp.zeros_like(acc_ref)

    mixed = (jax.nn.sigmoid(g0_ref[...]) * _dot(yp_ref[...], wp_ref[...])
             + jax.nn.sigmoid(g1_ref[...]) * _dot(ys_ref[...], ws_ref[...])
             + jax.nn.sigmoid(g2_ref[...]) * _dot(yh_ref[...], wh_ref[...]))
    acc_ref[...] += _dot(mixed.astype(_BF16), wo_ref[...])

    @pl.when(n == pl.num_programs(1) - 1)
    def _():
        o_ref[...] = x_ref[...] + acc_ref[...]


def merge_out(x, proj, y_pool, y_sb, y_hg, w_pool, w_sb, w_hg, w_out, tm=512, tn=512):
    T, D = x.shape
    gate0 = COL_GL * LANE // tn
    per_gate = D // tn

    def gate(idx):
        return pl.BlockSpec((tm, tn), lambda i, n: (i, gate0 + idx * per_gate + n))

    return pl.pallas_call(
        _merge_kernel,
        grid=(T // tm, D // tn),
        in_specs=[pl.BlockSpec((tm, D), lambda i, n: (i, 0)),
                  pl.BlockSpec((tm, POOL_WIDTH), lambda i, n: (i, 0)),
                  pl.BlockSpec((tm, SB_WIDTH), lambda i, n: (i, 0)),
                  pl.BlockSpec((tm, HG_WIDTH), lambda i, n: (i, 0)),
                  gate(0), gate(1), gate(2),
                  pl.BlockSpec((POOL_WIDTH, tn), lambda i, n: (0, n)),
                  pl.BlockSpec((SB_WIDTH, tn), lambda i, n: (0, n)),
                  pl.BlockSpec((HG_WIDTH, tn), lambda i, n: (0, n)),
                  pl.BlockSpec((tn, D), lambda i, n: (n, 0))],
        out_specs=pl.BlockSpec((tm, D), lambda i, n: (i, 0)),
        out_shape=jax.ShapeDtypeStruct((T, D), _F32),
        scratch_shapes=[pltpu.VMEM((tm, D), _F32)],
        compiler_params=_params("arbitrary", "arbitrary"),
        name="merge_out_proj",
    )(x, y_pool, y_sb, y_hg, proj, proj, proj, w_pool, w_sb, w_hg, w_out)


def _ffn_up_kernel(x_ref, g_ref, wg_ref, wu_ref, o_ref, hn_ref):
    @pl.when(pl.program_id(1) == 0)
    def _():
        hn_ref[...] = _rms_rows(x_ref[...], g_ref[...]).astype(_BF16)

    hn = hn_ref[...]
    o_ref[...] = (_silu(_dot(hn, wg_ref[...])) * _dot(hn, wu_ref[...])).astype(o_ref.dtype)


def ffn_up(x, g, w_gate_up, tm=1024, tn=512):
    T, D = x.shape
    nj = D_FF // tn
    return pl.pallas_call(
        _ffn_up_kernel,
        grid=(T // tm, nj),
        in_specs=[pl.BlockSpec((tm, D), lambda i, j: (i, 0)),
                  pl.BlockSpec((1, D), lambda i, j: (0, 0)),
                  pl.BlockSpec((D, tn), lambda i, j: (0, j)),
                  pl.BlockSpec((D, tn), lambda i, j: (0, nj + j))],
        out_specs=pl.BlockSpec((tm, tn), lambda i, j: (i, j)),
        out_shape=jax.ShapeDtypeStruct((T, D_FF), _BF16),
        scratch_shapes=[pltpu.VMEM((tm, D), _BF16)],
        compiler_params=_params("arbitrary", "arbitrary"),
        name="ffn_up",
    )(x, g.reshape(1, D), w_gate_up, w_gate_up)


def _ffn_down_kernel(x_ref, a_ref, w_ref, o_ref):
    o_ref[...] = x_ref[...] + _dot(a_ref[...], w_ref[...])


def ffn_down(x, act, w_down, tm=1024, tn=256):
    T, D = x.shape
    return pl.pallas_call(
        _ffn_down_kernel,
        grid=(T // tm, D // tn),
        in_specs=[pl.BlockSpec((tm, tn), lambda i, j: (i, j)),
                  pl.BlockSpec((tm, D_FF), lambda i, j: (i, 0)),
                  pl.BlockSpec((D_FF, tn), lambda i, j: (0, j))],
        out_specs=pl.BlockSpec((tm, tn), lambda i, j: (i, j)),
        out_shape=jax.ShapeDtypeStruct((T, D), _F32),
        compiler_params=_params("arbitrary", "arbitrary"),
        name="ffn_down",
    )(x, act, w_down)


def _ple_kernel(tn, x_ref, g_ref, p_ref, wg_ref, wp_ref, o_ref, hn_ref):
    j = pl.program_id(1)

    @pl.when(j == 0)
    def _():
        hn_ref[...] = _rms_rows(x_ref[...], g_ref[...]).astype(_BF16)

    gate = jax.nn.sigmoid(_dot(hn_ref[...], wg_ref[...]))
    emb = _dot(p_ref[...].astype(_BF16), wp_ref[...])
    o_ref[...] = x_ref[:, pl.ds(pl.multiple_of(j * tn, tn), tn)] + gate * emb


def ple_gate(x, g, p, w_gate, w_proj, tm=1024, tn=512):
    T, D = x.shape
    return pl.pallas_call(
        functools.partial(_ple_kernel, tn),
        grid=(T // tm, D // tn),
        in_specs=[pl.BlockSpec((tm, D), lambda i, j: (i, 0)),
                  pl.BlockSpec((1, D), lambda i, j: (0, 0)),
                  pl.BlockSpec((tm, PLE_DIM), lambda i, j: (i, 0)),
                  pl.BlockSpec((D, tn), lambda i, j: (0, j)),
                  pl.BlockSpec((PLE_DIM, tn), lambda i, j: (0, j))],
        out_specs=pl.BlockSpec((tm, tn), lambda i, j: (i, j)),
        out_shape=jax.ShapeDtypeStruct((T, D), _F32),
        scratch_shapes=[pltpu.VMEM((tm, D), _BF16)],
        compiler_params=_params("arbitrary", "arbitrary"),
        name="ple_gate",
    )(x, g.reshape(1, D), p, w_gate, w_proj)


def _final_norm_kernel(x_ref, g_ref, o_ref):
    o_ref[...] = _rms_rows(x_ref[...], g_ref[...])


def final_norm(x, g, tm=512):
    T, D = x.shape
    return pl.pallas_call(
        _final_norm_kernel,
        grid=(T // tm,),
        in_specs=[pl.BlockSpec((tm, D), lambda i: (i, 0)),
                  pl.BlockSpec((1, D), lambda i: (0, 0))],
        out_specs=pl.BlockSpec((tm, D), lambda i: (i, 0)),
        out_shape=jax.ShapeDtypeStruct((T, D), _F32),
        compiler_params=_params("arbitrary"),
        name="final_norm",
    )(x, g.reshape(1, D))


def kernel(x, p, norm_mix, w_in, pool_w, pool_scale, hg_lb, hg_norm, w_br_pool, w_br_sb, w_br_hg,
           w_out, norm_ffn, w_gate_up, w_down, norm_ple, w_ple_gate, w_ple_proj, norm_final):
    B, S, D = x.shape
    T = B * S
    depth = w_in.shape[0]
    xf = x.reshape(T, D)
    for i in range(depth):
        proj = norm_matmul(xf, norm_mix[i], w_in[i].astype(_BF16))
        proj3 = proj.reshape(B, S, IN_COLS)
        y_pool = pool_mixer(proj3, pool_w[i].astype(_BF16), pool_scale[i])
        y_sb = stick_breaking(proj3)
        y_hg = hgrn2_mixer(proj3, hg_lb, hg_norm[i], i)
        xf = merge_out(xf, proj, y_pool.reshape(T, POOL_WIDTH), y_sb.reshape(T, SB_WIDTH),
                       y_hg.reshape(T, HG_WIDTH), w_br_pool[i].astype(_BF16), w_br_sb[i].astype(_BF16),
                       w_br_hg[i].astype(_BF16), w_out[i].astype(_BF16))
        act = ffn_up(xf, norm_ffn[i], w_gate_up[i].astype(_BF16))
        xf = ffn_down(xf, act, w_down[i].astype(_BF16))
        xf = ple_gate(xf, norm_ple[i], p[i].reshape(T, PLE_DIM), w_ple_gate[i].astype(_BF16),
                      w_ple_proj[i].astype(_BF16))
    return final_norm(xf, norm_final).reshape(B, S, D)
```

```python
import functools

import jax
import jax.numpy as jnp
from jax import lax
from jax.experimental import pallas as pl
from jax.experimental.pallas import tpu as pltpu

D_MODEL = 2048
POOL_WIDTH = 512
POOL_WINDOWS = (2, 4, 8, 16)
POOL_GROUP_DIM = 128
SB_WIDTH = 1024
SB_HEAD_DIM = 128
SB_HEADS = 8
HG_WIDTH = 512
HG_HEAD_DIM = 128
HG_HEADS = 4
LB_FLOOR = 1e-20
D_FF = 5632
PLE_DIM = 256
EPS = 1e-6
IN_COLS = POOL_WIDTH + 3 * SB_WIDTH + 4 * HG_WIDTH + 3 * D_MODEL

LANE = 128

COL_POOL = 0
COL_SQ = POOL_WIDTH // LANE
COL_SK = COL_SQ + SB_WIDTH // LANE
COL_SV = COL_SK + SB_WIDTH // LANE
COL_ZF = COL_SV + SB_WIDTH // LANE
COL_HV = COL_ZF + HG_WIDTH // LANE
COL_HQ = COL_HV + HG_WIDTH // LANE
COL_OG = COL_HQ + HG_WIDTH // LANE
COL_GL = COL_OG + HG_WIDTH // LANE

ATT_BLOCK = 128
ATT_GROUP = 4
ATT_HEADS = 4
_LOG2E = 1.4426950408889634
HG_CHUNK = 128
HG_BAND = 16
VMEM_LIMIT = 52 * 1024 * 1024

_F32 = jnp.float32
_BF16 = jnp.bfloat16


def _params(*sem):
    return pltpu.CompilerParams(dimension_semantics=sem, vmem_limit_bytes=VMEM_LIMIT)


def _dot(a, b):
    return jnp.dot(a, b, preferred_element_type=_F32)


def _dot_nt(a, b):
    return lax.dot_general(a, b, (((1,), (1,)), ((), ())), preferred_element_type=_F32)


def _dot_tn(a, b):
    return lax.dot_general(a, b, (((0,), (0,)), ((), ())), preferred_element_type=_F32)


def _split_dot(lhs_fn, x):
    hi = x.astype(_BF16)
    lo = (x - hi.astype(_F32)).astype(_BF16)
    return lhs_fn(hi) + lhs_fn(lo)


def _rms_rows(x, g):
    return x * lax.rsqrt(jnp.mean(x * x, axis=-1, keepdims=True) + EPS) * g


def _silu(x):
    return x * jax.nn.sigmoid(x)


def _norm_matmul_kernel(x_ref, g_ref, w_ref, o_ref, hn_ref):
    @pl.when(pl.program_id(1) == 0)
    def _():
        hn_ref[...] = _rms_rows(x_ref[...], g_ref[...]).astype(_BF16)

    o_ref[...] = _dot(hn_ref[...], w_ref[...])


def norm_matmul(x, g, w, tm=1024, tn=512):
    T, D = x.shape
    N = w.shape[1]
    return pl.pallas_call(
        _norm_matmul_kernel,
        grid=(T // tm, N // tn),
        in_specs=[pl.BlockSpec((tm, D), lambda i, j: (i, 0)),
                  pl.BlockSpec((1, D), lambda i, j: (0, 0)),
                  pl.BlockSpec((D, tn), lambda i, j: (0, j))],
        out_specs=pl.BlockSpec((tm, tn), lambda i, j: (i, j)),
        out_shape=jax.ShapeDtypeStruct((T, N), _F32),
        scratch_shapes=[pltpu.VMEM((tm, D), _BF16)],
        compiler_params=_params("arbitrary", "arbitrary"),
        name="norm_in_proj",
    )(x, g.reshape(1, D), w)


def _pool_kernel(u_ref, w_ref, s_ref, o_ref):
    S = u_ref.shape[0]
    t = lax.broadcasted_iota(jnp.int32, (S, POOL_GROUP_DIM), 0)
    for gi, win in enumerate(POOL_WINDOWS):
        cols = slice(gi * POOL_GROUP_DIM, (gi + 1) * POOL_GROUP_DIM)
        u = u_ref[:, cols]
        acc = u
        shift = 1
        while shift < win:
            acc = acc + jnp.where(t >= shift, pltpu.roll(acc, shift, axis=0), 0.0)
            shift *= 2
        cnt = jnp.minimum(t + 1, win).astype(_F32)
        mixed = (acc / cnt - u).astype(_BF16)
        y = _dot(mixed, w_ref[gi]) * s_ref[:, cols]
        o_ref[:, cols] = y.astype(o_ref.dtype)


def pool_mixer(proj3, pool_w, pool_scale):
    B, S, _ = proj3.shape
    return pl.pallas_call(
        _pool_kernel,
        grid=(B,),
        in_specs=[pl.BlockSpec((None, S, POOL_WIDTH), lambda b: (b, 0, COL_POOL)),
                  pl.BlockSpec((len(POOL_WINDOWS), POOL_GROUP_DIM, POOL_GROUP_DIM), lambda b: (0, 0, 0)),
                  pl.BlockSpec((1, POOL_WIDTH), lambda b: (0, 0))],
        out_specs=pl.BlockSpec((None, S, POOL_WIDTH), lambda b: (b, 0, 0)),
        out_shape=jax.ShapeDtypeStruct((B, S, POOL_WIDTH), _BF16),
        compiler_params=_params("arbitrary"),
        name="pool_mixer",
    )(proj3, pool_w, pool_scale.reshape(1, POOL_WIDTH))


def _sb_kernel(q_ref, k_ref, v_ref, o_ref, kb_ref, vb_ref):
    i = pl.program_id(2)
    blk, group, heads = ATT_BLOCK, ATT_GROUP, ATT_HEADS
    scale = SB_HEAD_DIM ** -0.5

    @pl.when(i == 0)
    def _():
        kb_ref[...] = k_ref[...].astype(_BF16)
        vb_ref[...] = v_ref[...].astype(_BF16)

    q_all = q_ref[...].astype(_BF16)
    key_minus_query = (lax.broadcasted_iota(jnp.int32, (blk, blk), 1)
                       - lax.broadcasted_iota(jnp.int32, (blk, blk), 0))
    r2 = lax.broadcasted_iota(jnp.int32, (2 * blk, 2 * blk), 0) % blk
    c2 = lax.broadcasted_iota(jnp.int32, (2 * blk, 2 * blk), 1)
    neg_sums = jnp.where((r2 >= c2) | (c2 >= blk), -1.0, 0.0).astype(_BF16)

    def body(step, carry):
        accs, laters = carry
        diag_mask = key_minus_query < jnp.where(step == 0, 0, blk)
        new_accs, new_laters = [], []
        for hd in range(heads):
            cols = slice(hd * SB_HEAD_DIM, (hd + 1) * SB_HEAD_DIM)
            q = q_all[:, cols]
            parts = []
            for u in range(group):
                j = i - group * step - u
                start = pl.multiple_of(jnp.maximum(j, 0) * blk, blk)
                d = _dot_nt(q, kb_ref[pl.ds(start, blk), cols])
                z = d * scale
                softplus = jnp.maximum(z, 0.0) + jnp.log(1.0 + jnp.exp2(jnp.abs(d) * (-scale * _LOG2E)))
                if u == 0:
                    softplus = jnp.where(diag_mask, softplus, 0.0)
                hi = softplus.astype(_BF16)
                lo = (softplus - hi.astype(_F32)).astype(_BF16)
                sums = _dot(jnp.concatenate([hi, lo], axis=1), neg_sums)
                parts.append((u, j, start, z, sums))
            acc, later_blocks = accs[hd], laters[hd]
            for u, j, start, z, sums in parts:
                a = jnp.exp(z + (sums[:, :blk] + later_blocks))
                vb = vb_ref[pl.ds(start, blk), cols]
                if u == 0:
                    a = jnp.where(diag_mask, a, 0.0)
                else:
                    vb = jnp.where(j >= 0, vb, jnp.zeros_like(vb))
                acc = acc + _dot(a.astype(_BF16), vb)
                later_blocks = later_blocks + sums[:, blk:]
            new_accs.append(acc)
            new_laters.append(later_blocks)
        return tuple(new_accs), tuple(new_laters)

    zeros = tuple(jnp.zeros((blk, blk), _F32) for _ in range(heads))
    accs, _ = lax.fori_loop(0, i // group + 1, body, (zeros, zeros))
    o_ref[...] = jnp.concatenate(accs, axis=1).astype(o_ref.dtype)


def stick_breaking(proj3):
    B, S, _ = proj3.shape
    blk, heads = ATT_BLOCK, ATT_HEADS
    width = heads * SB_HEAD_DIM
    return pl.pallas_call(
        _sb_kernel,
        grid=(B, SB_HEADS // heads, S // blk),
        in_specs=[pl.BlockSpec((None, blk, width), lambda b, h, i: (b, i, COL_SQ // heads + h)),
                  pl.BlockSpec((None, S, width), lambda b, h, i: (b, 0, COL_SK // heads + h)),
                  pl.BlockSpec((None, S, width), lambda b, h, i: (b, 0, COL_SV // heads + h))],
        out_specs=pl.BlockSpec((None, blk, width), lambda b, h, i: (b, i, h)),
        out_shape=jax.ShapeDtypeStruct((B, S, SB_WIDTH), _BF16),
        scratch_shapes=[pltpu.VMEM((S, width), _BF16)] * 2,
        compiler_params=_params("arbitrary", "arbitrary", "arbitrary"),
        name="stick_breaking",
    )(proj3, proj3, proj3)


def _hg_kernel(layer, zf_ref, hv_ref, hq_ref, og_ref, lb_ref, gn_ref, o_ref,
               state_ref, kpad_ref, bpad_ref, vpad_ref):
    C, W, pad = HG_CHUNK, HG_HEAD_DIM, HG_BAND

    @pl.when(pl.program_id(2) == 0)
    def _():
        state_ref[...] = jnp.zeros_like(state_ref)

    lbs = [lb_ref[d] for d in range(lb_ref.shape[0])]
    top = functools.reduce(jnp.maximum, lbs)
    es = [jnp.exp(row - top) for row in lbs]
    total = functools.reduce(jnp.add, es)
    sm = [e / total for e in es]
    lb = jnp.clip(functools.reduce(jnp.add, sm[:layer + 1]) - sm[0], 0.0, 1.0)

    z = zf_ref[...]
    abs_z = jnp.log1p(jnp.exp(-jnp.abs(z)))
    log_sig = jnp.minimum(z, 0.0) - abs_z
    la = jnp.log(jnp.maximum(lb, LB_FLOOR))
    lc = jnp.log1p(-jnp.minimum(lb, 1.0 - 1e-6)) + log_sig
    log_f = jnp.maximum(la, lc) + jnp.log1p(jnp.exp(-jnp.abs(la - lc)))
    k = (1.0 - lb) * jax.nn.sigmoid(-z)
    q = _silu(hq_ref[...])
    v = hv_ref[...]

    r = lax.broadcasted_iota(jnp.int32, (C, C), 0)
    c = lax.broadcasted_iota(jnp.int32, (C, C), 1)
    tri = jnp.where(c <= r, 1.0, 0.0).astype(_BF16)
    b = _split_dot(lambda part: _dot(tri, part), log_f)

    state = state_ref[...]
    o = _dot_nt((q * jnp.exp(b)).astype(_BF16), state.astype(_BF16))

    scores = jnp.zeros((C, C), _F32)
    m = C // 2
    while m >= HG_BAND:
        span = 2 * m
        ref = jnp.concatenate(
            [jnp.broadcast_to(b[g * span + m - 1:g * span + m, :], (span, W)) for g in range(C // span)], axis=0)
        qt = q * jnp.exp(jnp.minimum(b - ref, 0.0))
        kt = k * jnp.exp(jnp.minimum(ref - b, 0.0))
        s_m = _dot_nt(qt.astype(_BF16), kt.astype(_BF16))
        sel = ((r // span) == (c // span)) & ((r % span) >= m) & ((c % span) < m)
        scores = jnp.where(sel, s_m, scores)
        m //= 2
    o = o + _dot(scores.astype(_BF16), v.astype(_BF16))

    zeros_pad = jnp.zeros((pad, W), _F32)
    for ref_, val in ((kpad_ref, k), (bpad_ref, b), (vpad_ref, v)):
        ref_[0:pad, :] = zeros_pad
        ref_[pad:pad + C, :] = val
    sub = lax.broadcasted_iota(jnp.int32, (C, 1), 0) % HG_BAND
    for d in range(HG_BAND):
        ks = kpad_ref[pad - d:pad - d + C, :]
        bs = bpad_ref[pad - d:pad - d + C, :]
        vs = vpad_ref[pad - d:pad - d + C, :]
        w = jnp.sum(q * ks * jnp.exp(jnp.minimum(b - bs, 0.0)), axis=-1, keepdims=True)
        o = o + jnp.where(sub >= d, w, 0.0) * vs

    b_last = b[C - 1:C, :]
    k_dec = k * jnp.exp(b_last - b)
    state_ref[...] = state * jnp.exp(b_last) + _dot_tn(v.astype(_BF16), k_dec.astype(_BF16))

    o = o * lax.rsqrt(jnp.mean(o * o, axis=-1, keepdims=True) + EPS)
    o_ref[...] = (o * gn_ref[...] * _silu(og_ref[...])).astype(o_ref.dtype)


def hgrn2_mixer(proj3, hg_lb, hg_norm_l, layer):
    B, S, _ = proj3.shape
    depth = hg_lb.shape[0]
    C, W = HG_CHUNK, HG_HEAD_DIM

    def col(base):
        return pl.BlockSpec((None, C, W), lambda b, h, c: (b, c, base + h))

    return pl.pallas_call(
        functools.partial(_hg_kernel, layer),
        grid=(B, HG_HEADS, S // C),
        in_specs=[col(COL_ZF), col(COL_HV), col(COL_HQ), col(COL_OG),
                  pl.BlockSpec((depth, None, 1, W), lambda b, h, c: (0, h, 0, 0)),
                  pl.BlockSpec((None, 1, W), lambda b, h, c: (h, 0, 0))],
        out_specs=pl.BlockSpec((None, C, W), lambda b, h, c: (b, c, h)),
        out_shape=jax.ShapeDtypeStruct((B, S, HG_WIDTH), _BF16),
        scratch_shapes=[pltpu.VMEM((W, W), _F32)] + [pltpu.VMEM((HG_BAND + C, W), _F32)] * 3,
        compiler_params=_params("arbitrary", "arbitrary", "arbitrary"),
        name="hgrn2_mixer",
    )(proj3, proj3, proj3, proj3,
      hg_lb.reshape(depth, HG_HEADS, 1, W), hg_norm_l.reshape(HG_HEADS, 1, W))


def _merge_kernel(x_ref, yp_ref, ys_ref, yh_ref, g0_ref, g1_ref, g2_ref,
                  wp_ref, ws_ref, wh_ref, wo_ref, o_ref, acc_ref):
    n = pl.program_id(1)

    @pl.when(n == 0)
    def _():
        acc_ref[...] = jnp.zeros_like(acc_ref)

    mixed = (jax.nn.sigmoid(g0_ref[...]) * _dot(yp_ref[...], wp_ref[...])
             + jax.nn.sigmoid(g1_ref[...]) * _dot(ys_ref[...], ws_ref[...])
             + jax.nn.sigmoid(g2_ref[...]) * _dot(yh_ref[...], wh_ref[...]))
    acc_ref[...] += _dot(mixed.astype(_BF16), wo_ref[...])

    @pl.when(n == pl.num_programs(1) - 1)
    def _():
        o_ref[...] = x_ref[...] + acc_ref[...]


def merge_out(x, proj, y_pool, y_sb, y_hg, w_pool, w_sb, w_hg, w_out, tm=512, tn=512):
    T, D = x.shape
    gate0 = COL_GL * LANE // tn
    per_gate = D // tn

    def gate(idx):
        return pl.BlockSpec((tm, tn), lambda i, n: (i, gate0 + idx * per_gate + n))

    return pl.pallas_call(
        _merge_kernel,
        grid=(T // tm, D // tn),
        in_specs=[pl.BlockSpec((tm, D), lambda i, n: (i, 0)),
                  pl.BlockSpec((tm, POOL_WIDTH), lambda i, n: (i, 0)),
                  pl.BlockSpec((tm, SB_WIDTH), lambda i, n: (i, 0)),
                  pl.BlockSpec((tm, HG_WIDTH), lambda i, n: (i, 0)),
                  gate(0), gate(1), gate(2),
                  pl.BlockSpec((POOL_WIDTH, tn), lambda i, n: (0, n)),
                  pl.BlockSpec((SB_WIDTH, tn), lambda i, n: (0, n)),
                  pl.BlockSpec((HG_WIDTH, tn), lambda i, n: (0, n)),
                  pl.BlockSpec((tn, D), lambda i, n: (n, 0))],
        out_specs=pl.BlockSpec((tm, D), lambda i, n: (i, 0)),
        out_shape=jax.ShapeDtypeStruct((T, D), _F32),
        scratch_shapes=[pltpu.VMEM((tm, D), _F32)],
        compiler_params=_params("arbitrary", "arbitrary"),
        name="merge_out_proj",
    )(x, y_pool, y_sb, y_hg, proj, proj, proj, w_pool, w_sb, w_hg, w_out)


def _ffn_up_kernel(x_ref, g_ref, wg_ref, wu_ref, o_ref, hn_ref):
    @pl.when(pl.program_id(1) == 0)
    def _():
        hn_ref[...] = _rms_rows(x_ref[...], g_ref[...]).astype(_BF16)

    hn = hn_ref[...]
    o_ref[...] = (_silu(_dot(hn, wg_ref[...])) * _dot(hn, wu_ref[...])).astype(o_ref.dtype)


def ffn_up(x, g, w_gate_up, tm=1024, tn=512):
    T, D = x.shape
    nj = D_FF // tn
    return pl.pallas_call(
        _ffn_up_kernel,
        grid=(T // tm, nj),
        in_specs=[pl.BlockSpec((tm, D), lambda i, j: (i, 0)),
                  pl.BlockSpec((1, D), lambda i, j: (0, 0)),
                  pl.BlockSpec((D, tn), lambda i, j: (0, j)),
                  pl.BlockSpec((D, tn), lambda i, j: (0, nj + j))],
        out_specs=pl.BlockSpec((tm, tn), lambda i, j: (i, j)),
        out_shape=jax.ShapeDtypeStruct((T, D_FF), _BF16),
        scratch_shapes=[pltpu.VMEM((tm, D), _BF16)],
        compiler_params=_params("arbitrary", "arbitrary"),
        name="ffn_up",
    )(x, g.reshape(1, D), w_gate_up, w_gate_up)


def _ffn_down_kernel(x_ref, a_ref, w_ref, o_ref):
    o_ref[...] = x_ref[...] + _dot(a_ref[...], w_ref[...])


def ffn_down(x, act, w_down, tm=1024, tn=256):
    T, D = x.shape
    return pl.pallas_call(
        _ffn_down_kernel,
        grid=(T // tm, D // tn),
        in_specs=[pl.BlockSpec((tm, tn), lambda i, j: (i, j)),
                  pl.BlockSpec((tm, D_FF), lambda i, j: (i, 0)),
                  pl.BlockSpec((D_FF, tn), lambda i, j: (0, j))],
        out_specs=pl.BlockSpec((tm, tn), lambda i, j: (i, j)),
        out_shape=jax.ShapeDtypeStruct((T, D), _F32),
        compiler_params=_params("arbitrary", "arbitrary"),
        name="ffn_down",
    )(x, act, w_down)


def _ple_kernel(tn, x_ref, g_ref, p_ref, wg_ref, wp_ref, o_ref, hn_ref):
    j = pl.program_id(1)

    @pl.when(j == 0)
    def _():
        hn_ref[...] = _rms_rows(x_ref[...], g_ref[...]).astype(_BF16)

    gate = jax.nn.sigmoid(_dot(hn_ref[...], wg_ref[...]))
    emb = _dot(p_ref[...].astype(_BF16), wp_ref[...])
    o_ref[...] = x_ref[:, pl.ds(pl.multiple_of(j * tn, tn), tn)] + gate * emb


def ple_gate(x, g, p, w_gate, w_proj, tm=1024, tn=512):
    T, D = x.shape
    return pl.pallas_call(
        functools.partial(_ple_kernel, tn),
        grid=(T // tm, D // tn),
        in_specs=[pl.BlockSpec((tm, D), lambda i, j: (i, 0)),
                  pl.BlockSpec((1, D), lambda i, j: (0, 0)),
                  pl.BlockSpec((tm, PLE_DIM), lambda i, j: (i, 0)),
                  pl.BlockSpec((D, tn), lambda i, j: (0, j)),
                  pl.BlockSpec((PLE_DIM, tn), lambda i, j: (0, j))],
        out_specs=pl.BlockSpec((tm, tn), lambda i, j: (i, j)),
        out_shape=jax.ShapeDtypeStruct((T, D), _F32),
        scratch_shapes=[pltpu.VMEM((tm, D), _BF16)],
        compiler_params=_params("arbitrary", "arbitrary"),
        name="ple_gate",
    )(x, g.reshape(1, D), p, w_gate, w_proj)


def _final_norm_kernel(x_ref, g_ref, o_ref):
    o_ref[...] = _rms_rows(x_ref[...], g_ref[...])


def final_norm(x, g, tm=512):
    T, D = x.shape
    return pl.pallas_call(
        _final_norm_kernel,
        grid=(T // tm,),
        in_specs=[pl.BlockSpec((tm, D), lambda i: (i, 0)),
                  pl.BlockSpec((1, D), lambda i: (0, 0))],
        out_specs=pl.BlockSpec((tm, D), lambda i: (i, 0)),
        out_shape=jax.ShapeDtypeStruct((T, D), _F32),
        compiler_params=_params("arbitrary"),
        name="final_norm",
    )(x, g.reshape(1, D))


def kernel(x, p, norm_mix, w_in, pool_w, pool_scale, hg_lb, hg_norm, w_br_pool, w_br_sb, w_br_hg,
           w_out, norm_ffn, w_gate_up, w_down, norm_ple, w_ple_gate, w_ple_proj, norm_final):
    B, S, D = x.shape
    T = B * S
    depth = w_in.shape[0]
    xf = x.reshape(T, D)
    for i in range(depth):
        proj = norm_matmul(xf, norm_mix[i], w_in[i].astype(_BF16))
        proj3 = proj.reshape(B, S, IN_COLS)
        y_pool = pool_mixer(proj3, pool_w[i].astype(_BF16), pool_scale[i])
        y_sb = stick_breaking(proj3)
        y_hg = hgrn2_mixer(proj3, hg_lb, hg_norm[i], i)
        xf = merge_out(xf, proj, y_pool.reshape(T, POOL_WIDTH), y_sb.reshape(T, SB_WIDTH),
                       y_hg.reshape(T, HG_WIDTH), w_br_pool[i].astype(_BF16), w_br_sb[i].astype(_BF16),
                       w_br_hg[i].astype(_BF16), w_out[i].astype(_BF16))
        act = ffn_up(xf, norm_ffn[i], w_gate_up[i].astype(_BF16))
        xf = ffn_down(xf, act, w_down[i].astype(_BF16))
        xf = ple_gate(xf, norm_ple[i], p[i].reshape(T, PLE_DIM), w_ple_gate[i].astype(_BF16),
                      w_ple_proj[i].astype(_BF16))
    return final_norm(xf, norm_final).reshape(B, S, D)
```

```python
import functools

import jax
import jax.numpy as jnp
from jax import lax
from jax.experimental import pallas as pl
from jax.experimental.pallas import tpu as pltpu

D_MODEL = 2048
POOL_WIDTH = 512
POOL_WINDOWS = (2, 4, 8, 16)
POOL_GROUP_DIM = 128
SB_WIDTH = 1024
SB_HEAD_DIM = 128
SB_HEADS = 8
HG_WIDTH = 512
HG_HEAD_DIM = 128
HG_HEADS = 4
LB_FLOOR = 1e-20
D_FF = 5632
PLE_DIM = 256
EPS = 1e-6
IN_COLS = POOL_WIDTH + 3 * SB_WIDTH + 4 * HG_WIDTH + 3 * D_MODEL

LANE = 128

QKV_WIDTH = 3 * SB_WIDTH
REST_WIDTH = IN_COLS - QKV_WIDTH
COL_POOL = 0
COL_ZF = POOL_WIDTH // LANE
COL_HV = COL_ZF + HG_WIDTH // LANE
COL_HQ = COL_HV + HG_WIDTH // LANE
COL_OG = COL_HQ + HG_WIDTH // LANE
COL_GL = COL_OG + HG_WIDTH // LANE

ATT_BLOCK = 128
ATT_GROUP = 3
ATT_HEADS = 8
_LOG2E = 1.4426950408889634
ATT_LOG_ZERO = -105.0
HG_CHUNK = 128
HG_BAND = 8
VMEM_LIMIT = 52 * 1024 * 1024

_F32 = jnp.float32
_BF16 = jnp.bfloat16


def _params(*sem):
    return pltpu.CompilerParams(dimension_semantics=sem, vmem_limit_bytes=VMEM_LIMIT)


def _dot(a, b):
    return jnp.dot(a, b, preferred_element_type=_F32)


def _dot_nt(a, b):
    return lax.dot_general(a, b, (((1,), (1,)), ((), ())), preferred_element_type=_F32)


def _dot_tn(a, b):
    return lax.dot_general(a, b, (((0,), (0,)), ((), ())), preferred_element_type=_F32)


def _split_dot(lhs_fn, x):
    hi = x.astype(_BF16)
    lo = (x - hi.astype(_F32)).astype(_BF16)
    return lhs_fn(hi) + lhs_fn(lo)


def _rms_rows(x, g):
    return x * lax.rsqrt(jnp.mean(x * x, axis=-1, keepdims=True) + EPS) * g


def _silu(x):
    return x * jax.nn.sigmoid(x)


def _in_proj_kernel(n_qkv, x_ref, g_ref, w_ref, qkv_ref, rest_ref, hn_ref):
    j = pl.program_id(1)

    @pl.when(j == 0)
    def _():
        hn_ref[...] = _rms_rows(x_ref[...], g_ref[...]).astype(_BF16)

    r = _dot(hn_ref[...], w_ref[...].astype(_BF16))

    @pl.when(j < n_qkv)
    def _():
        qkv_ref[...] = r.astype(_BF16)

    @pl.when(j >= n_qkv)
    def _():
        rest_ref[...] = r


def _layer_row(layer):
    return lambda *_: (layer, 0, 0)


def norm_in_proj(x, gains, w, layer, tm=2048, tn=512):
    T, D = x.shape
    n_pool, n_qkv, n_all = POOL_WIDTH // tn, QKV_WIDTH // tn, IN_COLS // tn

    def w_tile(i, j):
        return layer, 0, jnp.where(j < n_qkv, j + n_pool, jnp.where(j < n_qkv + n_pool, j - n_qkv, j))

    return pl.pallas_call(
        functools.partial(_in_proj_kernel, n_qkv),
        grid=(T // tm, n_all),
        in_specs=[pl.BlockSpec((tm, D), lambda i, j: (i, 0), pipeline_mode=pl.Buffered(1)),
                  pl.BlockSpec((None, 1, D), _layer_row(layer)),
                  pl.BlockSpec((None, D, tn), w_tile)],
        out_specs=[pl.BlockSpec((tm, tn), lambda i, j: (i, jnp.minimum(j, n_qkv - 1))),
                   pl.BlockSpec((tm, tn), lambda i, j: (i, jnp.maximum(j - n_qkv, 0)))],
        out_shape=[jax.ShapeDtypeStruct((T, QKV_WIDTH), _BF16),
                   jax.ShapeDtypeStruct((T, REST_WIDTH), _F32)],
        scratch_shapes=[pltpu.VMEM((tm, D), _BF16)],
        compiler_params=_params("arbitrary", "arbitrary"),
        name="norm_in_proj",
    )(x, gains.reshape(-1, 1, D), w)


def _pool_kernel(u_ref, w_ref, s_ref, o_ref):
    S = u_ref.shape[0]
    t = lax.broadcasted_iota(jnp.int32, (S, POOL_GROUP_DIM), 0)
    for gi, win in enumerate(POOL_WINDOWS):
        cols = slice(gi * POOL_GROUP_DIM, (gi + 1) * POOL_GROUP_DIM)
        u = u_ref[:, cols]
        acc = u
        shift = 1
        while shift < win:
            acc = acc + jnp.where(t >= shift, pltpu.roll(acc, shift, axis=0), 0.0)
            shift *= 2
        cnt = jnp.minimum(t + 1, win).astype(_F32)
        mixed = (acc / cnt - u).astype(_BF16)
        y = _dot(mixed, w_ref[gi]) * s_ref[:, cols]
        o_ref[:, cols] = y.astype(o_ref.dtype)


def pool_mixer(proj3, pool_w, pool_scale):
    B, S, _ = proj3.shape
    return pl.pallas_call(
        _pool_kernel,
        grid=(B,),
        in_specs=[pl.BlockSpec((None, S, POOL_WIDTH), lambda b: (b, 0, COL_POOL)),
                  pl.BlockSpec((len(POOL_WINDOWS), POOL_GROUP_DIM, POOL_GROUP_DIM), lambda b: (0, 0, 0)),
                  pl.BlockSpec((1, POOL_WIDTH), lambda b: (0, 0))],
        out_specs=pl.BlockSpec((None, S, POOL_WIDTH), lambda b: (b, 0, 0)),
        out_shape=jax.ShapeDtypeStruct((B, S, POOL_WIDTH), _BF16),
        compiler_params=_params("arbitrary"),
        name="pool_mixer",
    )(proj3, pool_w, pool_scale.reshape(1, POOL_WIDTH))


def _sb_kernel(q_ref, k_ref, v_ref, o_ref):
    i = pl.program_id(2)
    blk, group, heads = ATT_BLOCK, ATT_GROUP, ATT_HEADS
    scale = SB_HEAD_DIM ** -0.5
    q_all = q_ref[...]
    key_minus_query = (lax.broadcasted_iota(jnp.int32, (blk, blk), 1)
                       - lax.broadcasted_iota(jnp.int32, (blk, blk), 0))
    r2 = lax.broadcasted_iota(jnp.int32, (2 * blk, 2 * blk), 0) % blk
    c2 = lax.broadcasted_iota(jnp.int32, (2 * blk, 2 * blk), 1)
    neg_sums = jnp.where((r2 >= c2) | (c2 >= blk), -1.0, 0.0).astype(_BF16)

    def body(carry):
        step, _, accs, laters = carry
        diag_mask = key_minus_query < jnp.where(step == 0, 0, blk)
        parts = []
        for hd in range(heads):
            cols = slice(hd * SB_HEAD_DIM, (hd + 1) * SB_HEAD_DIM)
            q = q_all[:, cols]
            for u in range(group):
                j = i - group * step - u
                start = pl.multiple_of(jnp.maximum(j, 0) * blk, blk)
                d = _dot_nt(q, k_ref[pl.ds(start, blk), cols])
                z = d * scale
                softplus = jnp.maximum(z, 0.0) + jnp.log(1.0 + jnp.exp2(jnp.abs(d) * (-scale * _LOG2E)))
                if u == 0:
                    softplus = jnp.where(diag_mask, softplus, 0.0)
                hi = softplus.astype(_BF16)
                lo = (softplus - hi.astype(_F32)).astype(_BF16)
                sums = _dot(jnp.concatenate([hi, lo], axis=1), neg_sums)
                parts.append((hd, u, j, start, z, sums))
        accs, laters = list(accs), list(laters)
        for hd, u, j, start, z, sums in parts:
            cols = slice(hd * SB_HEAD_DIM, (hd + 1) * SB_HEAD_DIM)
            a = jnp.exp(z + (sums[:, :blk] + laters[hd]))
            vb = v_ref[pl.ds(start, blk), cols]
            if u == 0:
                a = jnp.where(diag_mask, a, 0.0)
            else:
                vb = jnp.where(j >= 0, vb, jnp.zeros_like(vb))
            accs[hd] = accs[hd] + _dot(a.astype(_BF16), vb)
            laters[hd] = laters[hd] + sums[:, blk:]
        largest_later = jnp.max(functools.reduce(jnp.maximum, laters))
        return step + 1, largest_later, tuple(accs), tuple(laters)

    def unfinished(carry):
        step, largest_later, _, _ = carry
        return (step <= i // group) & (largest_later > ATT_LOG_ZERO)

    zeros = tuple(jnp.zeros((blk, blk), _F32) for _ in range(heads))
    _, _, accs, _ = lax.while_loop(unfinished, body, (jnp.int32(0), jnp.float32(0.0), zeros, zeros))
    o_ref[...] = jnp.concatenate(accs, axis=1).astype(o_ref.dtype)


def stick_breaking(qkv3):
    B, S, _ = qkv3.shape
    blk, heads = ATT_BLOCK, ATT_HEADS
    width = heads * SB_HEAD_DIM
    per = SB_WIDTH // width
    return pl.pallas_call(
        _sb_kernel,
        grid=(B, per, S // blk),
        in_specs=[pl.BlockSpec((None, blk, width), lambda b, h, i: (b, i, h)),
                  pl.BlockSpec((None, S, width), lambda b, h, i: (b, 0, per + h)),
                  pl.BlockSpec((None, S, width), lambda b, h, i: (b, 0, 2 * per + h))],
        out_specs=pl.BlockSpec((None, blk, width), lambda b, h, i: (b, i, h)),
        out_shape=jax.ShapeDtypeStruct((B, S, SB_WIDTH), _BF16),
        compiler_params=_params("arbitrary", "arbitrary", "arbitrary"),
        name="stick_breaking",
    )(qkv3, qkv3, qkv3)


def _hg_kernel(layer, zf_ref, hv_ref, hq_ref, og_ref, lb_ref, gn_ref, o_ref,
               state_ref, kpad_ref, bpad_ref, vpad_ref):
    C, W, pad = HG_CHUNK, HG_HEAD_DIM, HG_BAND
    heads = [slice(h * W, (h + 1) * W) for h in range(HG_HEADS)]

    @pl.when(pl.program_id(1) == 0)
    def _():
        state_ref[...] = jnp.zeros_like(state_ref)

    lbs = [lb_ref[d] for d in range(lb_ref.shape[0])]
    top = functools.reduce(jnp.maximum, lbs)
    es = [jnp.exp(row - top) for row in lbs]
    total = functools.reduce(jnp.add, es)
    sm = [e / total for e in es]
    lb = jnp.clip(functools.reduce(jnp.add, sm[:layer + 1]) - sm[0], 0.0, 1.0)

    z = zf_ref[...]
    t = jnp.exp(-jnp.abs(z))
    big = 1.0 / (1.0 + t)
    small = t * big
    sig_pos = jnp.where(z >= 0.0, big, small)
    sig_neg = jnp.where(z >= 0.0, small, big)
    log_f = jnp.log(jnp.maximum(lb, LB_FLOOR) + (1.0 - jnp.minimum(lb, 1.0 - 1e-6)) * sig_pos)
    k = (1.0 - lb) * sig_neg
    q = _silu(hq_ref[...])
    v = hv_ref[...]
    v_bf = v.astype(_BF16)

    r = lax.broadcasted_iota(jnp.int32, (C, C), 0)
    c = lax.broadcasted_iota(jnp.int32, (C, C), 1)
    tri = jnp.where(c <= r, 1.0, 0.0).astype(_BF16)
    b = _split_dot(lambda part: _dot(tri, part), log_f) * _LOG2E

    states = [state_ref[h] for h in range(HG_HEADS)]
    q_dec = (q * jnp.exp2(b)).astype(_BF16)
    outs = [_dot_nt(q_dec[:, hs], states[h].astype(_BF16)) for h, hs in enumerate(heads)]

    scores = [jnp.zeros((C, C), _F32) for _ in heads]
    m = C // 2
    while m >= HG_BAND:
        span = 2 * m
        ref = jnp.concatenate(
            [jnp.broadcast_to(b[g * span + m - 1:g * span + m, :], (span, HG_WIDTH)) for g in range(C // span)],
            axis=0)
        qt = (q * jnp.exp2(jnp.minimum(b - ref, 0.0))).astype(_BF16)
        kt = (k * jnp.exp2(jnp.minimum(ref - b, 0.0))).astype(_BF16)
        sel = ((r // span) == (c // span)) & ((r % span) >= m) & ((c % span) < m)
        scores = [jnp.where(sel, _dot_nt(qt[:, hs], kt[:, hs]), scores[h]) for h, hs in enumerate(heads)]
        m //= 2
    outs = [outs[h] + _dot(scores[h].astype(_BF16), v_bf[:, hs]) for h, hs in enumerate(heads)]

    zeros_pad = jnp.zeros((pad, W), _F32)
    for ref_, val in ((kpad_ref, k), (bpad_ref, b), (vpad_ref, v)):
        for h, hs in enumerate(heads):
            ref_[h, 0:pad, :] = zeros_pad
            ref_[h, pad:pad + C, :] = val[:, hs]
    sub = lax.broadcasted_iota(jnp.int32, (C, 1), 0) % HG_BAND
    for d in range(HG_BAND):
        rows = slice(pad - d, pad - d + C)
        for h, hs in enumerate(heads):
            decay = jnp.exp2(jnp.minimum(b[:, hs] - bpad_ref[h, rows, :], 0.0))
            w = jnp.sum(q[:, hs] * kpad_ref[h, rows, :] * decay, axis=-1, keepdims=True)
            outs[h] = outs[h] + jnp.where(sub >= d, w, 0.0) * vpad_ref[h, rows, :]

    b_last = b[C - 1:C, :]
    k_dec = (k * jnp.exp2(b_last - b)).astype(_BF16)
    keep = jnp.exp2(b_last)
    for h, hs in enumerate(heads):
        state_ref[h] = states[h] * keep[:, hs] + _dot_tn(v_bf[:, hs], k_dec[:, hs])

    normed = [o * lax.rsqrt(jnp.mean(o * o, axis=-1, keepdims=True) + EPS) for o in outs]
    o_ref[...] = (jnp.concatenate(normed, axis=1) * gn_ref[...] * _silu(og_ref[...])).astype(o_ref.dtype)


def hgrn2_mixer(rest3, hg_lb, hg_norm, layer):
    B, S, _ = rest3.shape
    depth = hg_lb.shape[0]
    C, W = HG_CHUNK, HG_HEAD_DIM

    def cols(base):
        return pl.BlockSpec((None, C, HG_WIDTH), lambda b, c: (b, c, base * LANE // HG_WIDTH))

    return pl.pallas_call(
        functools.partial(_hg_kernel, layer),
        grid=(B, S // C),
        in_specs=[cols(COL_ZF), cols(COL_HV), cols(COL_HQ), cols(COL_OG),
                  pl.BlockSpec((depth, 1, HG_WIDTH), lambda b, c: (0, 0, 0)),
                  pl.BlockSpec((None, 1, HG_WIDTH), lambda b, c: (layer, 0, 0))],
        out_specs=pl.BlockSpec((None, C, HG_WIDTH), lambda b, c: (b, c, 0)),
        out_shape=jax.ShapeDtypeStruct((B, S, HG_WIDTH), _BF16),
        scratch_shapes=[pltpu.VMEM((HG_HEADS, W, W), _F32)] + [pltpu.VMEM((HG_HEADS, HG_BAND + C, W), _F32)] * 3,
        compiler_params=_params("arbitrary", "arbitrary"),
        name="hgrn2_mixer",
    )(rest3, rest3, rest3, rest3, hg_lb.reshape(depth, 1, HG_WIDTH), hg_norm.reshape(depth, 1, HG_WIDTH))


def _merge_kernel(x_ref, yp_ref, ys_ref, yh_ref, g0_ref, g1_ref, g2_ref,
                  wp_ref, ws_ref, wh_ref, wo_ref, o_ref, acc_ref):
    n = pl.program_id(1)

    @pl.when(n == 0)
    def _():
        acc_ref[...] = jnp.zeros_like(acc_ref)

    mixed = (jax.nn.sigmoid(g0_ref[...]) * _dot(yp_ref[...], wp_ref[...])
             + jax.nn.sigmoid(g1_ref[...]) * _dot(ys_ref[...], ws_ref[...])
             + jax.nn.sigmoid(g2_ref[...]) * _dot(yh_ref[...], wh_ref[...]))
    acc_ref[...] += _dot(mixed.astype(_BF16), wo_ref[...])

    @pl.when(n == pl.num_programs(1) - 1)
    def _():
        o_ref[...] = x_ref[...] + acc_ref[...]


def merge_out(x, proj, y_pool, y_sb, y_hg, w_pool, w_sb, w_hg, w_out, tm=512, tn=512):
    T, D = x.shape
    gate0 = COL_GL * LANE // tn
    per_gate = D // tn

    def gate(idx):
        return pl.BlockSpec((tm, tn), lambda i, n: (i, gate0 + idx * per_gate + n))

    return pl.pallas_call(
        _merge_kernel,
        grid=(T // tm, D // tn),
        in_specs=[pl.BlockSpec((tm, D), lambda i, n: (i, 0)),
                  pl.BlockSpec((tm, POOL_WIDTH), lambda i, n: (i, 0)),
                  pl.BlockSpec((tm, SB_WIDTH), lambda i, n: (i, 0)),
                  pl.BlockSpec((tm, HG_WIDTH), lambda i, n: (i, 0)),
                  gate(0), gate(1), gate(2),
                  pl.BlockSpec((POOL_WIDTH, tn), lambda i, n: (0, n)),
                  pl.BlockSpec((SB_WIDTH, tn), lambda i, n: (0, n)),
                  pl.BlockSpec((HG_WIDTH, tn), lambda i, n: (0, n)),
                  pl.BlockSpec((tn, D), lambda i, n: (n, 0))],
        out_specs=pl.BlockSpec((tm, D), lambda i, n: (i, 0)),
        out_shape=jax.ShapeDtypeStruct((T, D), _F32),
        scratch_shapes=[pltpu.VMEM((tm, D), _F32)],
        compiler_params=_params("arbitrary", "arbitrary"),
        name="merge_out_proj",
    )(x, y_pool, y_sb, y_hg, proj, proj, proj, w_pool, w_sb, w_hg, w_out)


def _ffn_up_kernel(x_ref, g_ref, wg_ref, wu_ref, o_ref, hn_ref):
    @pl.when(pl.program_id(1) == 0)
    def _():
        hn_ref[...] = _rms_rows(x_ref[...], g_ref[...]).astype(_BF16)

    hn = hn_ref[...]
    gate = _dot(hn, wg_ref[...].astype(_BF16))
    up = _dot(hn, wu_ref[...].astype(_BF16))
    o_ref[...] = (_silu(gate) * up).astype(o_ref.dtype)


def ffn_up(x, gains, w_gate_up, layer, tm=2048, tn=256):
    T, D = x.shape
    nj = D_FF // tn
    return pl.pallas_call(
        _ffn_up_kernel,
        grid=(T // tm, nj),
        in_specs=[pl.BlockSpec((tm, D), lambda i, j: (i, 0), pipeline_mode=pl.Buffered(1)),
                  pl.BlockSpec((None, 1, D), _layer_row(layer)),
                  pl.BlockSpec((None, D, tn), lambda i, j: (layer, 0, j)),
                  pl.BlockSpec((None, D, tn), lambda i, j: (layer, 0, nj + j))],
        out_specs=pl.BlockSpec((tm, tn), lambda i, j: (i, j)),
        out_shape=jax.ShapeDtypeStruct((T, D_FF), _BF16),
        scratch_shapes=[pltpu.VMEM((tm, D), _BF16)],
        compiler_params=_params("arbitrary", "arbitrary"),
        name="ffn_up",
    )(x, gains.reshape(-1, 1, D), w_gate_up, w_gate_up)


def _ffn_down_kernel(x_ref, a_ref, w_ref, o_ref):
    o_ref[...] = x_ref[...] + _dot(a_ref[...], w_ref[...])


def ffn_down(x, act, w_down, tm=1024, tn=256):
    T, D = x.shape
    return pl.pallas_call(
        _ffn_down_kernel,
        grid=(T // tm, D // tn),
        in_specs=[pl.BlockSpec((tm, tn), lambda i, j: (i, j)),
                  pl.BlockSpec((tm, D_FF), lambda i, j: (i, 0)),
                  pl.BlockSpec((D_FF, tn), lambda i, j: (0, j))],
        out_specs=pl.BlockSpec((tm, tn), lambda i, j: (i, j)),
        out_shape=jax.ShapeDtypeStruct((T, D), _F32),
        compiler_params=_params("arbitrary", "arbitrary"),
        name="ffn_down",
    )(x, act, w_down)


def _ple_kernel(tn, x_ref, g_ref, p_ref, wg_ref, wp_ref, o_ref, hn_ref):
    j = pl.program_id(1)

    @pl.when(j == 0)
    def _():
        hn_ref[...] = _rms_rows(x_ref[...], g_ref[...]).astype(_BF16)

    gate = jax.nn.sigmoid(_dot(hn_ref[...], wg_ref[...]))
    emb = _dot(p_ref[...].astype(_BF16), wp_ref[...])
    o_ref[...] = x_ref[:, pl.ds(pl.multiple_of(j * tn, tn), tn)] + gate * emb


def ple_gate(x, g, p, w_gate, w_proj, tm=1024, tn=512):
    T, D = x.shape
    return pl.pallas_call(
        functools.partial(_ple_kernel, tn),
        grid=(T // tm, D // tn),
        in_specs=[pl.BlockSpec((tm, D), lambda i, j: (i, 0)),
                  pl.BlockSpec((1, D), lambda i, j: (0, 0)),
                  pl.BlockSpec((tm, PLE_DIM), lambda i, j: (i, 0)),
                  pl.BlockSpec((D, tn), lambda i, j: (0, j)),
                  pl.BlockSpec((PLE_DIM, tn), lambda i, j: (0, j))],
        out_specs=pl.BlockSpec((tm, tn), lambda i, j: (i, j)),
        out_shape=jax.ShapeDtypeStruct((T, D), _F32),
        scratch_shapes=[pltpu.VMEM((tm, D), _BF16)],
        compiler_params=_params("arbitrary", "arbitrary"),
        name="ple_gate",
    )(x, g.reshape(1, D), p, w_gate, w_proj)


def _final_norm_kernel(x_ref, g_ref, o_ref):
    o_ref[...] = _rms_rows(x_ref[...], g_ref[...])


def final_norm(x, g, tm=512):
    T, D = x.shape
    return pl.pallas_call(
        _final_norm_kernel,
        grid=(T // tm,),
        in_specs=[pl.BlockSpec((tm, D), lambda i: (i, 0)),
                  pl.BlockSpec((1, D), lambda i: (0, 0))],
        out_specs=pl.BlockSpec((tm, D), lambda i: (i, 0)),
        out_shape=jax.ShapeDtypeStruct((T, D), _F32),
        compiler_params=_params("arbitrary"),
        name="final_norm",
    )(x, g.reshape(1, D))


def kernel(x, p, norm_mix, w_in, pool_w, pool_scale, hg_lb, hg_norm, w_br_pool, w_br_sb, w_br_hg,
           w_out, norm_ffn, w_gate_up, w_down, norm_ple, w_ple_gate, w_ple_proj, norm_final):
    B, S, D = x.shape
    T = B * S
    depth = w_in.shape[0]
    xf = x.reshape(T, D)
    for i in range(depth):
        qkv, rest = norm_in_proj(xf, norm_mix, w_in, i)
        rest3 = rest.reshape(B, S, REST_WIDTH)
        y_pool = pool_mixer(rest3, pool_w[i].astype(_BF16), pool_scale[i])
        y_sb = stick_breaking(qkv.reshape(B, S, QKV_WIDTH))
        y_hg = hgrn2_mixer(rest3, hg_lb, hg_norm, i)
        xf = merge_out(xf, rest, y_pool.reshape(T, POOL_WIDTH), y_sb.reshape(T, SB_WIDTH),
                       y_hg.reshape(T, HG_WIDTH), w_br_pool[i].astype(_BF16), w_br_sb[i].astype(_BF16),
                       w_br_hg[i].astype(_BF16), w_out[i].astype(_BF16))
        act = ffn_up(xf, norm_ffn, w_gate_up, i)
        xf = ffn_down(xf, act, w_down[i].astype(_BF16))
        xf = ple_gate(xf, norm_ple[i], p[i].reshape(T, PLE_DIM), w_ple_gate[i].astype(_BF16),
                      w_ple_proj[i].astype(_BF16))
    return final_norm(xf, norm_final).reshape(B, S, D)
```

```python
import functools

import jax
import jax.numpy as jnp
from jax import lax
from jax.experimental import pallas as pl
from jax.experimental.pallas import tpu as pltpu

D_MODEL = 2048
POOL_WIDTH = 512
POOL_WINDOWS = (2, 4, 8, 16)
POOL_GROUP_DIM = 128
SB_WIDTH = 1024
SB_HEAD_DIM = 128
SB_HEADS = 8
HG_WIDTH = 512
HG_HEAD_DIM = 128
HG_HEADS = 4
LB_FLOOR = 1e-20
D_FF = 5632
PLE_DIM = 256
EPS = 1e-6
IN_COLS = POOL_WIDTH + 3 * SB_WIDTH + 4 * HG_WIDTH + 3 * D_MODEL

LANE = 128

QKV_WIDTH = 3 * SB_WIDTH
REST_WIDTH = IN_COLS - QKV_WIDTH
COL_POOL = 0
COL_ZF = POOL_WIDTH // LANE
COL_HV = COL_ZF + HG_WIDTH // LANE
COL_HQ = COL_HV + HG_WIDTH // LANE
COL_OG = COL_HQ + HG_WIDTH // LANE
COL_GL = COL_OG + HG_WIDTH // LANE

ATT_BLOCK = 128
ATT_GROUP = 3
ATT_HEADS = 8
_LOG2E = 1.4426950408889634
ATT_LOG_ZERO = -105.0
HG_CHUNK = 128
HG_BAND = 8
VMEM_LIMIT = 52 * 1024 * 1024
IN_PROJ_TILE = 512

_F32 = jnp.float32
_BF16 = jnp.bfloat16


def _params(*sem):
    return pltpu.CompilerParams(dimension_semantics=sem, vmem_limit_bytes=VMEM_LIMIT)


def _dot(a, b):
    return jnp.dot(a, b, preferred_element_type=_F32)


def _dot_nt(a, b):
    return lax.dot_general(a, b, (((1,), (1,)), ((), ())), preferred_element_type=_F32)


def _dot_tn(a, b):
    return lax.dot_general(a, b, (((0,), (0,)), ((), ())), preferred_element_type=_F32)


def _split_dot(lhs_fn, x):
    hi = x.astype(_BF16)
    lo = (x - hi.astype(_F32)).astype(_BF16)
    return lhs_fn(hi) + lhs_fn(lo)


def _rms_rows(x, g):
    return x * lax.rsqrt(jnp.mean(x * x, axis=-1, keepdims=True) + EPS) * g


def _silu(x):
    return x * jax.nn.sigmoid(x)


def _in_proj_kernel(x_ref, g_ref, w_ref, qkv_ref, rest_ref, hn_ref):
    @pl.when(pl.program_id(1) == 0)
    def _():
        hn_ref[...] = _rms_rows(x_ref[...], g_ref[...]).astype(_BF16)

    r = _dot(hn_ref[...], w_ref[...].astype(_BF16))
    qkv_ref[...] = r.astype(_BF16)
    rest_ref[...] = r


def _layer_row(layer):
    return lambda *_: (layer, 0, 0)


def norm_in_proj(x, gains, w, layer, tm=2048, tn=IN_PROJ_TILE):
    T, D = x.shape
    n_pool, n_qkv, n_all = POOL_WIDTH // tn, QKV_WIDTH // tn, IN_COLS // tn
    n_rest = n_all - n_qkv

    def w_tile(i, j):
        return layer, 0, jnp.where(j < n_qkv, j + n_pool, jnp.where(j < n_qkv + n_pool, j - n_qkv, j))

    def qkv_tile(i, j):
        return i, jnp.minimum(j, n_qkv)

    def rest_tile(i, j):
        return i, jnp.where(j < n_qkv, n_rest, j - n_qkv)

    return pl.pallas_call(
        _in_proj_kernel,
        grid=(T // tm, n_all),
        in_specs=[pl.BlockSpec((tm, D), lambda i, j: (i, 0), pipeline_mode=pl.Buffered(1)),
                  pl.BlockSpec((None, 1, D), _layer_row(layer)),
                  pl.BlockSpec((None, D, tn), w_tile)],
        out_specs=[pl.BlockSpec((tm, tn), qkv_tile), pl.BlockSpec((tm, tn), rest_tile)],
        out_shape=[jax.ShapeDtypeStruct((T, QKV_WIDTH + tn), _BF16),
                   jax.ShapeDtypeStruct((T, REST_WIDTH + tn), _F32)],
        scratch_shapes=[pltpu.VMEM((tm, D), _BF16)],
        compiler_params=_params("arbitrary", "arbitrary"),
        name="norm_in_proj",
    )(x, gains.reshape(-1, 1, D), w)


def _pool_kernel(u_ref, w_ref, s_ref, o_ref):
    S = u_ref.shape[0]
    t = lax.broadcasted_iota(jnp.int32, (S, POOL_GROUP_DIM), 0)
    for gi, win in enumerate(POOL_WINDOWS):
        cols = slice(gi * POOL_GROUP_DIM, (gi + 1) * POOL_GROUP_DIM)
        u = u_ref[:, cols]
        acc = u
        shift = 1
        while shift < win:
            acc = acc + jnp.where(t >= shift, pltpu.roll(acc, shift, axis=0), 0.0)
            shift *= 2
        cnt = jnp.minimum(t + 1, win).astype(_F32)
        mixed = (acc / cnt - u).astype(_BF16)
        y = _dot(mixed, w_ref[gi]) * s_ref[:, cols]
        o_ref[:, cols] = y.astype(o_ref.dtype)


def pool_mixer(proj3, pool_w, pool_scale):
    B, S, _ = proj3.shape
    return pl.pallas_call(
        _pool_kernel,
        grid=(B,),
        in_specs=[pl.BlockSpec((None, S, POOL_WIDTH), lambda b: (b, 0, COL_POOL)),
                  pl.BlockSpec((len(POOL_WINDOWS), POOL_GROUP_DIM, POOL_GROUP_DIM), lambda b: (0, 0, 0)),
                  pl.BlockSpec((1, POOL_WIDTH), lambda b: (0, 0))],
        out_specs=pl.BlockSpec((None, S, POOL_WIDTH), lambda b: (b, 0, 0)),
        out_shape=jax.ShapeDtypeStruct((B, S, POOL_WIDTH), _BF16),
        compiler_params=_params("arbitrary"),
        name="pool_mixer",
    )(proj3, pool_w, pool_scale.reshape(1, POOL_WIDTH))


def _sb_kernel(q_ref, k_ref, v_ref, o_ref):
    i = pl.program_id(2)
    blk, group, heads = ATT_BLOCK, ATT_GROUP, ATT_HEADS
    scale = SB_HEAD_DIM ** -0.5
    q_all = q_ref[...]
    key_minus_query = (lax.broadcasted_iota(jnp.int32, (blk, blk), 1)
                       - lax.broadcasted_iota(jnp.int32, (blk, blk), 0))
    r2 = lax.broadcasted_iota(jnp.int32, (2 * blk, 2 * blk), 0) % blk
    c2 = lax.broadcasted_iota(jnp.int32, (2 * blk, 2 * blk), 1)
    neg_sums = jnp.where((r2 >= c2) | (c2 >= blk), -1.0, 0.0).astype(_BF16)

    def body(carry):
        step, _, accs, laters = carry
        diag_mask = key_minus_query < jnp.where(step == 0, 0, blk)
        parts = []
        for hd in range(heads):
            cols = slice(hd * SB_HEAD_DIM, (hd + 1) * SB_HEAD_DIM)
            q = q_all[:, cols]
            for u in range(group):
                j = i - group * step - u
                start = pl.multiple_of(jnp.maximum(j, 0) * blk, blk)
                d = _dot_nt(q, k_ref[pl.ds(start, blk), cols])
                z = d * scale
                softplus = jnp.maximum(z, 0.0) + jnp.log(1.0 + jnp.exp2(jnp.abs(d) * (-scale * _LOG2E)))
                if u == 0:
                    softplus = jnp.where(diag_mask, softplus, 0.0)
                hi = softplus.astype(_BF16)
                lo = (softplus - hi.astype(_F32)).astype(_BF16)
                sums = _dot(jnp.concatenate([hi, lo], axis=1), neg_sums)
                parts.append((hd, u, j, start, z, sums))
        accs, laters = list(accs), list(laters)
        for hd, u, j, start, z, sums in parts:
            cols = slice(hd * SB_HEAD_DIM, (hd + 1) * SB_HEAD_DIM)
            a = jnp.exp(z + (sums[:, :blk] + laters[hd]))
            vb = v_ref[pl.ds(start, blk), cols]
            if u == 0:
                a = jnp.where(diag_mask, a, 0.0)
            else:
                vb = jnp.where(j >= 0, vb, jnp.zeros_like(vb))
            accs[hd] = accs[hd] + _dot(a.astype(_BF16), vb)
            laters[hd] = laters[hd] + sums[:, blk:]
        largest_later = jnp.max(functools.reduce(jnp.maximum, laters))
        return step + 1, largest_later, tuple(accs), tuple(laters)

    def unfinished(carry):
        step, largest_later, _, _ = carry
        return (step <= i // group) & (largest_later > ATT_LOG_ZERO)

    zeros = tuple(jnp.zeros((blk, blk), _F32) for _ in range(heads))
    _, _, accs, _ = lax.while_loop(unfinished, body, (jnp.int32(0), jnp.float32(0.0), zeros, zeros))
    o_ref[...] = jnp.concatenate(accs, axis=1).astype(o_ref.dtype)


def stick_breaking(qkv3):
    B, S, _ = qkv3.shape
    blk, heads = ATT_BLOCK, ATT_HEADS
    width = heads * SB_HEAD_DIM
    per = SB_WIDTH // width
    return pl.pallas_call(
        _sb_kernel,
        grid=(B, per, S // blk),
        in_specs=[pl.BlockSpec((None, blk, width), lambda b, h, i: (b, i, h)),
                  pl.BlockSpec((None, S, width), lambda b, h, i: (b, 0, per + h)),
                  pl.BlockSpec((None, S, width), lambda b, h, i: (b, 0, 2 * per + h))],
        out_specs=pl.BlockSpec((None, blk, width), lambda b, h, i: (b, i, h)),
        out_shape=jax.ShapeDtypeStruct((B, S, SB_WIDTH), _BF16),
        compiler_params=_params("arbitrary", "arbitrary", "arbitrary"),
        name="stick_breaking",
    )(qkv3, qkv3, qkv3)


def _hg_kernel(layer, zf_ref, hv_ref, hq_ref, og_ref, lb_ref, gn_ref, o_ref,
               state_ref, kpad_ref, bpad_ref, vpad_ref):
    C, W, pad = HG_CHUNK, HG_HEAD_DIM, HG_BAND
    heads = [slice(h * W, (h + 1) * W) for h in range(HG_HEADS)]

    @pl.when(pl.program_id(1) == 0)
    def _():
        state_ref[...] = jnp.zeros_like(state_ref)

    lbs = [lb_ref[d] for d in range(lb_ref.shape[0])]
    top = functools.reduce(jnp.maximum, lbs)
    es = [jnp.exp(row - top) for row in lbs]
    total = functools.reduce(jnp.add, es)
    sm = [e / total for e in es]
    lb = jnp.clip(functools.reduce(jnp.add, sm[:layer + 1]) - sm[0], 0.0, 1.0)

    z = zf_ref[...]
    t = jnp.exp(-jnp.abs(z))
    big = 1.0 / (1.0 + t)
    small = t * big
    sig_pos = jnp.where(z >= 0.0, big, small)
    sig_neg = jnp.where(z >= 0.0, small, big)
    log_f = jnp.log(jnp.maximum(lb, LB_FLOOR) + (1.0 - jnp.minimum(lb, 1.0 - 1e-6)) * sig_pos)
    k = (1.0 - lb) * sig_neg
    q = _silu(hq_ref[...])
    v = hv_ref[...]
    v_bf = v.astype(_BF16)

    r = lax.broadcasted_iota(jnp.int32, (C, C), 0)
    c = lax.broadcasted_iota(jnp.int32, (C, C), 1)
    tri = jnp.where(c <= r, 1.0, 0.0).astype(_BF16)
    b = _split_dot(lambda part: _dot(tri, part), log_f) * _LOG2E

    states = [state_ref[h] for h in range(HG_HEADS)]
    q_dec = (q * jnp.exp2(b)).astype(_BF16)
    outs = [_dot_nt(q_dec[:, hs], states[h].astype(_BF16)) for h, hs in enumerate(heads)]

    scores = [jnp.zeros((C, C), _F32) for _ in heads]
    m = C // 2
    while m >= HG_BAND:
        span = 2 * m
        ref = jnp.concatenate(
            [jnp.broadcast_to(b[g * span + m - 1:g * span + m, :], (span, HG_WIDTH)) for g in range(C // span)],
            axis=0)
        qt = (q * jnp.exp2(jnp.minimum(b - ref, 0.0))).astype(_BF16)
        kt = (k * jnp.exp2(jnp.minimum(ref - b, 0.0))).astype(_BF16)
        sel = ((r // span) == (c // span)) & ((r % span) >= m) & ((c % span) < m)
        scores = [jnp.where(sel, _dot_nt(qt[:, hs], kt[:, hs]), scores[h]) for h, hs in enumerate(heads)]
        m //= 2
    outs = [outs[h] + _dot(scores[h].astype(_BF16), v_bf[:, hs]) for h, hs in enumerate(heads)]

    zeros_pad = jnp.zeros((pad, W), _F32)
    for ref_, val in ((kpad_ref, k), (bpad_ref, b), (vpad_ref, v)):
        for h, hs in enumerate(heads):
            ref_[h, 0:pad, :] = zeros_pad
            ref_[h, pad:pad + C, :] = val[:, hs]
    sub = lax.broadcasted_iota(jnp.int32, (C, 1), 0) % HG_BAND
    for d in range(HG_BAND):
        rows = slice(pad - d, pad - d + C)
        for h, hs in enumerate(heads):
            decay = jnp.exp2(jnp.minimum(b[:, hs] - bpad_ref[h, rows, :], 0.0))
            w = jnp.sum(q[:, hs] * kpad_ref[h, rows, :] * decay, axis=-1, keepdims=True)
            outs[h] = outs[h] + jnp.where(sub >= d, w, 0.0) * vpad_ref[h, rows, :]

    b_last = b[C - 1:C, :]
    k_dec = (k * jnp.exp2(b_last - b)).astype(_BF16)
    keep = jnp.exp2(b_last)
    for h, hs in enumerate(heads):
        state_ref[h] = states[h] * keep[:, hs] + _dot_tn(v_bf[:, hs], k_dec[:, hs])

    normed = [o * lax.rsqrt(jnp.mean(o * o, axis=-1, keepdims=True) + EPS) for o in outs]
    o_ref[...] = (jnp.concatenate(normed, axis=1) * gn_ref[...] * _silu(og_ref[...])).astype(o_ref.dtype)


def hgrn2_mixer(rest3, hg_lb, hg_norm, layer):
    B, S, _ = rest3.shape
    depth = hg_lb.shape[0]
    C, W = HG_CHUNK, HG_HEAD_DIM

    def cols(base):
        return pl.BlockSpec((None, C, HG_WIDTH), lambda b, c: (b, c, base * LANE // HG_WIDTH))

    return pl.pallas_call(
        functools.partial(_hg_kernel, layer),
        grid=(B, S // C),
        in_specs=[cols(COL_ZF), cols(COL_HV), cols(COL_HQ), cols(COL_OG),
                  pl.BlockSpec((depth, 1, HG_WIDTH), lambda b, c: (0, 0, 0)),
                  pl.BlockSpec((None, 1, HG_WIDTH), lambda b, c: (layer, 0, 0))],
        out_specs=pl.BlockSpec((None, C, HG_WIDTH), lambda b, c: (b, c, 0)),
        out_shape=jax.ShapeDtypeStruct((B, S, HG_WIDTH), _BF16),
        scratch_shapes=[pltpu.VMEM((HG_HEADS, W, W), _F32)] + [pltpu.VMEM((HG_HEADS, HG_BAND + C, W), _F32)] * 3,
        compiler_params=_params("arbitrary", "arbitrary"),
        name="hgrn2_mixer",
    )(rest3, rest3, rest3, rest3, hg_lb.reshape(depth, 1, HG_WIDTH), hg_norm.reshape(depth, 1, HG_WIDTH))


def _branch_mix_kernel(yp_ref, ys_ref, yh_ref, g0_ref, g1_ref, g2_ref, wp_ref, ws_ref, wh_ref, o_ref):
    mixed = (jax.nn.sigmoid(g0_ref[...]) * _dot(yp_ref[...], wp_ref[...].astype(_BF16))
             + jax.nn.sigmoid(g1_ref[...]) * _dot(ys_ref[...], ws_ref[...].astype(_BF16))
             + jax.nn.sigmoid(g2_ref[...]) * _dot(yh_ref[...], wh_ref[...].astype(_BF16)))
    o_ref[...] = mixed.astype(o_ref.dtype)


def branch_mix(rest, y_pool, y_sb, y_hg, w_pool, w_sb, w_hg, layer, tm=1024, tn=512):
    T = rest.shape[0]
    D = D_MODEL
    gate0 = COL_GL * LANE // tn
    per_gate = D // tn

    def gate(idx):
        return pl.BlockSpec((tm, tn), lambda i, n: (i, gate0 + idx * per_gate + n))

    def weight(width):
        return pl.BlockSpec((None, width, tn), lambda i, n: (layer, 0, n))

    return pl.pallas_call(
        _branch_mix_kernel,
        grid=(T // tm, D // tn),
        in_specs=[pl.BlockSpec((tm, POOL_WIDTH), lambda i, n: (i, 0)),
                  pl.BlockSpec((tm, SB_WIDTH), lambda i, n: (i, 0)),
                  pl.BlockSpec((tm, HG_WIDTH), lambda i, n: (i, 0)),
                  gate(0), gate(1), gate(2),
                  weight(POOL_WIDTH), weight(SB_WIDTH), weight(HG_WIDTH)],
        out_specs=pl.BlockSpec((tm, tn), lambda i, n: (i, n)),
        out_shape=jax.ShapeDtypeStruct((T, D), _BF16),
        compiler_params=_params("arbitrary", "arbitrary"),
        name="branch_mix",
    )(y_pool, y_sb, y_hg, rest, rest, rest, w_pool, w_sb, w_hg)


def _out_proj_kernel(rows, x_ref, m_ref, w_ref, g_ref, o_ref, hn_ref):
    w = w_ref[...]
    for r0 in range(0, x_ref.shape[0], rows):
        rs = slice(r0, r0 + rows)
        out = x_ref[rs, :] + _dot(m_ref[rs, :], w)
        o_ref[rs, :] = out
        hn_ref[rs, :] = _rms_rows(out, g_ref[...]).astype(hn_ref.dtype)


def out_proj(x, mixed, w_out_bf, gains, layer, tm=512, rows=256):
    T, D = x.shape
    return pl.pallas_call(
        functools.partial(_out_proj_kernel, rows),
        grid=(T // tm,),
        in_specs=[pl.BlockSpec((tm, D), lambda i: (i, 0)),
                  pl.BlockSpec((tm, D), lambda i: (i, 0)),
                  pl.BlockSpec((D, D), lambda i: (0, 0), pipeline_mode=pl.Buffered(1)),
                  pl.BlockSpec((None, 1, D), _layer_row(layer))],
        out_specs=[pl.BlockSpec((tm, D), lambda i: (i, 0)), pl.BlockSpec((tm, D), lambda i: (i, 0))],
        out_shape=[jax.ShapeDtypeStruct((T, D), _F32), jax.ShapeDtypeStruct((T, D), _BF16)],
        compiler_params=_params("arbitrary"),
        name="out_proj",
    )(x, mixed, w_out_bf, gains.reshape(-1, 1, D))


def _ffn_up_kernel(h_ref, wg_ref, wu_ref, o_ref):
    h = h_ref[...]
    gate = _dot(h, wg_ref[...].astype(_BF16))
    up = _dot(h, wu_ref[...].astype(_BF16))
    o_ref[...] = (_silu(gate) * up).astype(o_ref.dtype)


def ffn_up(h, w_gate_up, layer, tm=2048, tn=512):
    T, D = h.shape
    nj = D_FF // tn
    return pl.pallas_call(
        _ffn_up_kernel,
        grid=(T // tm, nj),
        in_specs=[pl.BlockSpec((tm, D), lambda i, j: (i, 0)),
                  pl.BlockSpec((None, D, tn), lambda i, j: (layer, 0, j)),
                  pl.BlockSpec((None, D, tn), lambda i, j: (layer, 0, nj + j))],
        out_specs=pl.BlockSpec((tm, tn), lambda i, j: (i, j)),
        out_shape=jax.ShapeDtypeStruct((T, D_FF), _BF16),
        compiler_params=_params("arbitrary", "arbitrary"),
        name="ffn_up",
    )(h, w_gate_up, w_gate_up)


def _ffn_down_kernel(x_ref, a_ref, w_ref, o_ref):
    o_ref[...] = x_ref[...] + _dot(a_ref[...], w_ref[...])


def ffn_down(x, act, w_down, tm=1024, tn=512):
    T, D = x.shape
    return pl.pallas_call(
        _ffn_down_kernel,
        grid=(T // tm, D // tn),
        in_specs=[pl.BlockSpec((tm, tn), lambda i, j: (i, j)),
                  pl.BlockSpec((tm, D_FF), lambda i, j: (i, 0)),
                  pl.BlockSpec((D_FF, tn), lambda i, j: (0, j))],
        out_specs=pl.BlockSpec((tm, tn), lambda i, j: (i, j)),
        out_shape=jax.ShapeDtypeStruct((T, D), _F32),
        compiler_params=_params("arbitrary", "arbitrary"),
        name="ffn_down",
    )(x, act, w_down)


def _ple_kernel(tn, x_ref, g_ref, p_ref, wg_ref, wp_ref, o_ref, hn_ref):
    j = pl.program_id(1)

    @pl.when(j == 0)
    def _():
        hn_ref[...] = _rms_rows(x_ref[...], g_ref[...]).astype(_BF16)

    gate = jax.nn.sigmoid(_dot(hn_ref[...], wg_ref[...]))
    emb = _dot(p_ref[...].astype(_BF16), wp_ref[...])
    o_ref[...] = x_ref[:, pl.ds(pl.multiple_of(j * tn, tn), tn)] + gate * emb


def ple_gate(x, g, p, w_gate, w_proj, tm=1024, tn=1024):
    T, D = x.shape
    return pl.pallas_call(
        functools.partial(_ple_kernel, tn),
        grid=(T // tm, D // tn),
        in_specs=[pl.BlockSpec((tm, D), lambda i, j: (i, 0)),
                  pl.BlockSpec((1, D), lambda i, j: (0, 0)),
                  pl.BlockSpec((tm, PLE_DIM), lambda i, j: (i, 0)),
                  pl.BlockSpec((D, tn), lambda i, j: (0, j)),
                  pl.BlockSpec((PLE_DIM, tn), lambda i, j: (0, j))],
        out_specs=pl.BlockSpec((tm, tn), lambda i, j: (i, j)),
        out_shape=jax.ShapeDtypeStruct((T, D), _F32),
        scratch_shapes=[pltpu.VMEM((tm, D), _BF16)],
        compiler_params=_params("arbitrary", "arbitrary"),
        name="ple_gate",
    )(x, g.reshape(1, D), p, w_gate, w_proj)


def _final_norm_kernel(x_ref, g_ref, o_ref):
    o_ref[...] = _rms_rows(x_ref[...], g_ref[...])


def final_norm(x, g, tm=512):
    T, D = x.shape
    return pl.pallas_call(
        _final_norm_kernel,
        grid=(T // tm,),
        in_specs=[pl.BlockSpec((tm, D), lambda i: (i, 0)),
                  pl.BlockSpec((1, D), lambda i: (0, 0))],
        out_specs=pl.BlockSpec((tm, D), lambda i: (i, 0)),
        out_shape=jax.ShapeDtypeStruct((T, D), _F32),
        compiler_params=_params("arbitrary"),
        name="final_norm",
    )(x, g.reshape(1, D))


def kernel(x, p, norm_mix, w_in, pool_w, pool_scale, hg_lb, hg_norm, w_br_pool, w_br_sb, w_br_hg,
           w_out, norm_ffn, w_gate_up, w_down, norm_ple, w_ple_gate, w_ple_proj, norm_final):
    B, S, D = x.shape
    T = B * S
    depth = w_in.shape[0]
    xf = x.reshape(T, D)
    for i in range(depth):
        qkv, rest = norm_in_proj(xf, norm_mix, w_in, i)
        rest3 = rest.reshape(B, S, -1)
        y_pool = pool_mixer(rest3, pool_w[i].astype(_BF16), pool_scale[i])
        y_sb = stick_breaking(qkv.reshape(B, S, -1))
        y_hg = hgrn2_mixer(rest3, hg_lb, hg_norm, i)
        mixed = branch_mix(rest, y_pool.reshape(T, POOL_WIDTH), y_sb.reshape(T, SB_WIDTH),
                           y_hg.reshape(T, HG_WIDTH), w_br_pool, w_br_sb, w_br_hg, i)
        xf, h_ffn = out_proj(xf, mixed, w_out[i].astype(_BF16), norm_ffn, i)
        act = ffn_up(h_ffn, w_gate_up, i)
        xf = ffn_down(xf, act, w_down[i].astype(_BF16))
        xf = ple_gate(xf, norm_ple[i], p[i].reshape(T, PLE_DIM), w_ple_gate[i].astype(_BF16),
                      w_ple_proj[i].astype(_BF16))
    return final_norm(xf, norm_final).reshape(B, S, D)
```

```python
import functools

import jax
import jax.numpy as jnp
from jax import lax
from jax.experimental import pallas as pl
from jax.experimental.pallas import tpu as pltpu

D_MODEL = 2048
POOL_WIDTH = 512
POOL_WINDOWS = (2, 4, 8, 16)
POOL_GROUP_DIM = 128
SB_WIDTH = 1024
SB_HEAD_DIM = 128
SB_HEADS = 8
HG_WIDTH = 512
HG_HEAD_DIM = 128
HG_HEADS = 4
LB_FLOOR = 1e-20
D_FF = 5632
PLE_DIM = 256
EPS = 1e-6
IN_COLS = POOL_WIDTH + 3 * SB_WIDTH + 4 * HG_WIDTH + 3 * D_MODEL

LANE = 128

N_BRANCH = 3
QKV_WIDTH = 3 * SB_WIDTH
ACT_WIDTH = QKV_WIDTH + N_BRANCH * D_MODEL
REST_WIDTH = IN_COLS - ACT_WIDTH
COL_POOL = 0
COL_ZF = POOL_WIDTH // LANE
COL_HV = COL_ZF + HG_WIDTH // LANE
COL_HQ = COL_HV + HG_WIDTH // LANE
COL_OG = COL_HQ + HG_WIDTH // LANE

ATT_BLOCK = 128
ATT_GROUP = 3
ATT_HEADS = 8
_LOG2E = 1.4426950408889634
ATT_LOG_ZERO = -105.0
HG_CHUNK = 128
HG_BAND = 4
HG_PAD = 8
VMEM_LIMIT = 56 * 1024 * 1024
IN_PROJ_TILE = 512

_F32 = jnp.float32
_BF16 = jnp.bfloat16


def _params(*sem):
    return pltpu.CompilerParams(dimension_semantics=sem, vmem_limit_bytes=VMEM_LIMIT)


def _dot(a, b):
    return jnp.dot(a, b, preferred_element_type=_F32)


def _dot_nt(a, b):
    return lax.dot_general(a, b, (((1,), (1,)), ((), ())), preferred_element_type=_F32)


def _dot_tn(a, b):
    return lax.dot_general(a, b, (((0,), (0,)), ((), ())), preferred_element_type=_F32)


def _split_dot(lhs_fn, x):
    hi = x.astype(_BF16)
    lo = (x - hi.astype(_F32)).astype(_BF16)
    return lhs_fn(hi) + lhs_fn(lo)


def _rms_rows(x, g):
    return x * lax.rsqrt(jnp.mean(x * x, axis=-1, keepdims=True) + EPS) * g


def _silu(x):
    return x * jax.nn.sigmoid(x)


def _in_proj_kernel(x_ref, g_ref, w_ref, act_ref, rest_ref, hn_ref):
    @pl.when(pl.program_id(1) == 0)
    def _():
        hn_ref[...] = _rms_rows(x_ref[...], g_ref[...]).astype(_BF16)

    r = _dot(hn_ref[...], w_ref[...].astype(_BF16))
    act_ref[...] = r.astype(_BF16)
    rest_ref[...] = r


def _layer_row(layer):
    return lambda *_: (layer, 0, 0)


def norm_in_proj(x, gains, w, layer, tm=2048, tn=IN_PROJ_TILE):
    T, D = x.shape
    n_pool, n_qkv, n_hg, n_gate = POOL_WIDTH // tn, QKV_WIDTH // tn, 4 * HG_WIDTH // tn, N_BRANCH * D // tn
    n_act, n_rest = n_qkv + n_gate, n_pool + n_hg

    def w_tile(i, j):
        tile = jnp.where(j < n_qkv, j + n_pool,
                         jnp.where(j < n_act, j + n_pool + n_hg,
                                   jnp.where(j < n_act + n_pool, j - n_act, j - n_gate)))
        return layer, 0, tile

    def act_tile(i, j):
        return i, jnp.minimum(j, n_act)

    def rest_tile(i, j):
        return i, jnp.where(j < n_act, n_rest, j - n_act)

    return pl.pallas_call(
        _in_proj_kernel,
        grid=(T // tm, n_act + n_rest),
        in_specs=[pl.BlockSpec((tm, D), lambda i, j: (i, 0), pipeline_mode=pl.Buffered(1)),
                  pl.BlockSpec((None, 1, D), _layer_row(layer)),
                  pl.BlockSpec((None, D, tn), w_tile)],
        out_specs=[pl.BlockSpec((tm, tn), act_tile), pl.BlockSpec((tm, tn), rest_tile)],
        out_shape=[jax.ShapeDtypeStruct((T, ACT_WIDTH + tn), _BF16),
                   jax.ShapeDtypeStruct((T, REST_WIDTH + tn), _F32)],
        scratch_shapes=[pltpu.VMEM((tm, D), _BF16)],
        compiler_params=_params("arbitrary", "arbitrary"),
        name="norm_in_proj",
    )(x, gains.reshape(-1, 1, D), w)


def _pool_kernel(u_ref, w_ref, s_ref, o_ref):
    S = u_ref.shape[0]
    t = lax.broadcasted_iota(jnp.int32, (S, POOL_GROUP_DIM), 0)
    for gi, win in enumerate(POOL_WINDOWS):
        cols = slice(gi * POOL_GROUP_DIM, (gi + 1) * POOL_GROUP_DIM)
        u = u_ref[:, cols]
        acc = u
        shift = 1
        while shift < win:
            acc = acc + jnp.where(t >= shift, pltpu.roll(acc, shift, axis=0), 0.0)
            shift *= 2
        cnt = jnp.minimum(t + 1, win).astype(_F32)
        mixed = (acc / cnt - u).astype(_BF16)
        y = _dot(mixed, w_ref[gi]) * s_ref[:, cols]
        o_ref[:, cols] = y.astype(o_ref.dtype)


def pool_mixer(proj3, pool_w, pool_scale):
    B, S, _ = proj3.shape
    return pl.pallas_call(
        _pool_kernel,
        grid=(B,),
        in_specs=[pl.BlockSpec((None, S, POOL_WIDTH), lambda b: (b, 0, COL_POOL)),
                  pl.BlockSpec((len(POOL_WINDOWS), POOL_GROUP_DIM, POOL_GROUP_DIM), lambda b: (0, 0, 0)),
                  pl.BlockSpec((1, POOL_WIDTH), lambda b: (0, 0))],
        out_specs=pl.BlockSpec((None, S, POOL_WIDTH), lambda b: (b, 0, 0)),
        out_shape=jax.ShapeDtypeStruct((B, S, POOL_WIDTH), _BF16),
        compiler_params=_params("arbitrary"),
        name="pool_mixer",
    )(proj3, pool_w, pool_scale.reshape(1, POOL_WIDTH))


def _sb_kernel(q_ref, k_ref, v_ref, o_ref):
    i = pl.program_id(2)
    blk, group, heads = ATT_BLOCK, ATT_GROUP, ATT_HEADS
    scale = SB_HEAD_DIM ** -0.5
    q_all = q_ref[...]
    key_minus_query = (lax.broadcasted_iota(jnp.int32, (blk, blk), 1)
                       - lax.broadcasted_iota(jnp.int32, (blk, blk), 0))
    r2 = lax.broadcasted_iota(jnp.int32, (2 * blk, 2 * blk), 0) % blk
    c2 = lax.broadcasted_iota(jnp.int32, (2 * blk, 2 * blk), 1)
    neg_sums = jnp.where((r2 >= c2) | (c2 >= blk), -1.0, 0.0).astype(_BF16)

    def body(carry):
        step, _, accs, laters = carry
        diag_mask = key_minus_query < jnp.where(step == 0, 0, blk)
        parts = []
        for hd in range(heads):
            cols = slice(hd * SB_HEAD_DIM, (hd + 1) * SB_HEAD_DIM)
            q = q_all[:, cols]
            for u in range(group):
                j = i - group * step - u
                start = pl.multiple_of(jnp.maximum(j, 0) * blk, blk)
                d = _dot_nt(q, k_ref[pl.ds(start, blk), cols])
                z = d * scale
                softplus = jnp.maximum(z, 0.0) + jnp.log(1.0 + jnp.exp2(jnp.abs(d) * (-scale * _LOG2E)))
                if u == 0:
                    softplus = jnp.where(diag_mask, softplus, 0.0)
                hi = softplus.astype(_BF16)
                lo = (softplus - hi.astype(_F32)).astype(_BF16)
                sums = _dot(jnp.concatenate([hi, lo], axis=1), neg_sums)
                parts.append((hd, u, j, start, z, sums))
        accs, laters = list(accs), list(laters)
        for hd, u, j, start, z, sums in parts:
            cols = slice(hd * SB_HEAD_DIM, (hd + 1) * SB_HEAD_DIM)
            a = jnp.exp(z + (sums[:, :blk] + laters[hd]))
            vb = v_ref[pl.ds(start, blk), cols]
            if u == 0:
                a = jnp.where(diag_mask, a, 0.0)
            else:
                vb = jnp.where(j >= 0, vb, jnp.zeros_like(vb))
            accs[hd] = accs[hd] + _dot(a.astype(_BF16), vb)
            laters[hd] = laters[hd] + sums[:, blk:]
        largest_later = jnp.max(functools.reduce(jnp.maximum, laters))
        return step + 1, largest_later, tuple(accs), tuple(laters)

    def unfinished(carry):
        step, largest_later, _, _ = carry
        return (step <= i // group) & (largest_later > ATT_LOG_ZERO)

    zeros = tuple(jnp.zeros((blk, blk), _F32) for _ in range(heads))
    _, _, accs, _ = lax.while_loop(unfinished, body, (jnp.int32(0), jnp.float32(0.0), zeros, zeros))
    o_ref[...] = jnp.concatenate(accs, axis=1).astype(o_ref.dtype)


def stick_breaking(qkv3):
    B, S, _ = qkv3.shape
    blk, heads = ATT_BLOCK, ATT_HEADS
    width = heads * SB_HEAD_DIM
    per = SB_WIDTH // width
    return pl.pallas_call(
        _sb_kernel,
        grid=(B, per, S // blk),
        in_specs=[pl.BlockSpec((None, blk, width), lambda b, h, i: (b, i, h)),
                  pl.BlockSpec((None, S, width), lambda b, h, i: (b, 0, per + h)),
                  pl.BlockSpec((None, S, width), lambda b, h, i: (b, 0, 2 * per + h))],
        out_specs=pl.BlockSpec((None, blk, width), lambda b, h, i: (b, i, h)),
        out_shape=jax.ShapeDtypeStruct((B, S, SB_WIDTH), _BF16),
        compiler_params=_params("arbitrary", "arbitrary", "arbitrary"),
        name="stick_breaking",
    )(qkv3, qkv3, qkv3)


def _hg_kernel(layer, zf_ref, hv_ref, hq_ref, og_ref, lb_ref, gn_ref, o_ref,
               state_ref, kpad_ref, bpad_ref, vpad_ref):
    C, W, pad = HG_CHUNK, HG_HEAD_DIM, HG_PAD
    heads = [slice(h * W, (h + 1) * W) for h in range(HG_HEADS)]

    @pl.when(pl.program_id(1) == 0)
    def _():
        state_ref[...] = jnp.zeros_like(state_ref)

    lbs = [lb_ref[d] for d in range(lb_ref.shape[0])]
    top = functools.reduce(jnp.maximum, lbs)
    es = [jnp.exp(row - top) for row in lbs]
    total = functools.reduce(jnp.add, es)
    sm = [e / total for e in es]
    lb = jnp.clip(functools.reduce(jnp.add, sm[:layer + 1]) - sm[0], 0.0, 1.0)

    z = zf_ref[...]
    t = jnp.exp(-jnp.abs(z))
    big = 1.0 / (1.0 + t)
    small = t * big
    sig_pos = jnp.where(z >= 0.0, big, small)
    sig_neg = jnp.where(z >= 0.0, small, big)
    log_f = jnp.log(jnp.maximum(lb, LB_FLOOR) + (1.0 - jnp.minimum(lb, 1.0 - 1e-6)) * sig_pos)
    k = (1.0 - lb) * sig_neg
    q = _silu(hq_ref[...])
    v = hv_ref[...]
    v_bf = v.astype(_BF16)

    r = lax.broadcasted_iota(jnp.int32, (C, C), 0)
    c = lax.broadcasted_iota(jnp.int32, (C, C), 1)
    tri = jnp.where(c <= r, 1.0, 0.0).astype(_BF16)
    b = _split_dot(lambda part: _dot(tri, part), log_f) * _LOG2E

    states = [state_ref[h] for h in range(HG_HEADS)]
    q_dec = (q * jnp.exp2(b)).astype(_BF16)
    outs = [_dot_nt(q_dec[:, hs], states[h].astype(_BF16)) for h, hs in enumerate(heads)]

    scores = [jnp.zeros((C, C), _F32) for _ in heads]
    m = C // 2
    while m >= HG_BAND:
        span = 2 * m
        ref = jnp.concatenate(
            [jnp.broadcast_to(b[g * span + m - 1:g * span + m, :], (span, HG_WIDTH)) for g in range(C // span)],
            axis=0)
        qt = (q * jnp.exp2(jnp.minimum(b - ref, 0.0))).astype(_BF16)
        kt = (k * jnp.exp2(jnp.minimum(ref - b, 0.0))).astype(_BF16)
        sel = ((r // span) == (c // span)) & ((r % span) >= m) & ((c % span) < m)
        scores = [jnp.where(sel, _dot_nt(qt[:, hs], kt[:, hs]), scores[h]) for h, hs in enumerate(heads)]
        m //= 2
    outs = [outs[h] + _dot(scores[h].astype(_BF16), v_bf[:, hs]) for h, hs in enumerate(heads)]

    zeros_pad = jnp.zeros((pad, W), _F32)
    for ref_, val in ((kpad_ref, k), (bpad_ref, b), (vpad_ref, v)):
        for h, hs in enumerate(heads):
            ref_[h, 0:pad, :] = zeros_pad
            ref_[h, pad:pad + C, :] = val[:, hs]
    sub = lax.broadcasted_iota(jnp.int32, (C, 1), 0) % HG_BAND
    for d in range(HG_BAND):
        rows = slice(pad - d, pad - d + C)
        for h, hs in enumerate(heads):
            decay = jnp.exp2(jnp.minimum(b[:, hs] - bpad_ref[h, rows, :], 0.0))
            w = jnp.sum(q[:, hs] * kpad_ref[h, rows, :] * decay, axis=-1, keepdims=True)
            outs[h] = outs[h] + jnp.where(sub >= d, w, 0.0) * vpad_ref[h, rows, :]

    b_last = b[C - 1:C, :]
    k_dec = (k * jnp.exp2(b_last - b)).astype(_BF16)
    keep = jnp.exp2(b_last)
    for h, hs in enumerate(heads):
        state_ref[h] = states[h] * keep[:, hs] + _dot_tn(v_bf[:, hs], k_dec[:, hs])

    normed = [o * lax.rsqrt(jnp.mean(o * o, axis=-1, keepdims=True) + EPS) for o in outs]
    o_ref[...] = (jnp.concatenate(normed, axis=1) * gn_ref[...] * _silu(og_ref[...])).astype(o_ref.dtype)


def hgrn2_mixer(rest3, hg_lb, hg_norm, layer):
    B, S, _ = rest3.shape
    depth = hg_lb.shape[0]
    C, W = HG_CHUNK, HG_HEAD_DIM

    def cols(base):
        return pl.BlockSpec((None, C, HG_WIDTH), lambda b, c: (b, c, base * LANE // HG_WIDTH))

    return pl.pallas_call(
        functools.partial(_hg_kernel, layer),
        grid=(B, S // C),
        in_specs=[cols(COL_ZF), cols(COL_HV), cols(COL_HQ), cols(COL_OG),
                  pl.BlockSpec((depth, 1, HG_WIDTH), lambda b, c: (0, 0, 0)),
                  pl.BlockSpec((None, 1, HG_WIDTH), lambda b, c: (layer, 0, 0))],
        out_specs=pl.BlockSpec((None, C, HG_WIDTH), lambda b, c: (b, c, 0)),
        out_shape=jax.ShapeDtypeStruct((B, S, HG_WIDTH), _BF16),
        scratch_shapes=[pltpu.VMEM((HG_HEADS, W, W), _F32)] + [pltpu.VMEM((HG_HEADS, HG_PAD + C, W), _F32)] * 3,
        compiler_params=_params("arbitrary", "arbitrary"),
        name="hgrn2_mixer",
    )(rest3, rest3, rest3, rest3, hg_lb.reshape(depth, 1, HG_WIDTH), hg_norm.reshape(depth, 1, HG_WIDTH))


def _branch_mix_kernel(yp_ref, ys_ref, yh_ref, g0_ref, g1_ref, g2_ref, wp_ref, ws_ref, wh_ref, slab_ref,
                       o_ref, slab_bf_ref):
    def gate(g_ref):
        return jax.nn.sigmoid(g_ref[...].astype(_F32))

    mixed = (gate(g0_ref) * _dot(yp_ref[...], wp_ref[...].astype(_BF16))
             + gate(g1_ref) * _dot(ys_ref[...], ws_ref[...].astype(_BF16))
             + gate(g2_ref) * _dot(yh_ref[...], wh_ref[...].astype(_BF16)))
    o_ref[...] = mixed.astype(o_ref.dtype)
    slab_bf_ref[...] = slab_ref[...].astype(_BF16)


def branch_mix(act, y_pool, y_sb, y_hg, w_pool, w_sb, w_hg, w_out, layer, tm=1024, tn=512):
    T = act.shape[0]
    D = D_MODEL
    gate0 = QKV_WIDTH // tn
    per_gate = D // tn
    slab_in, slab_out, slab_shape = _cast_slab_specs(w_out, layer, (T // tm) * per_gate,
                                                     lambda i, n: i * per_gate + n)

    def gate(idx):
        return pl.BlockSpec((tm, tn), lambda i, n: (i, gate0 + idx * per_gate + n))

    def weight(width):
        return pl.BlockSpec((None, width, tn), lambda i, n: (layer, 0, n))

    return pl.pallas_call(
        _branch_mix_kernel,
        grid=(T // tm, D // tn),
        in_specs=[pl.BlockSpec((tm, POOL_WIDTH), lambda i, n: (i, 0)),
                  pl.BlockSpec((tm, SB_WIDTH), lambda i, n: (i, 0)),
                  pl.BlockSpec((tm, HG_WIDTH), lambda i, n: (i, 0)),
                  gate(0), gate(1), gate(2),
                  weight(POOL_WIDTH), weight(SB_WIDTH), weight(HG_WIDTH), slab_in],
        out_specs=[pl.BlockSpec((tm, tn), lambda i, n: (i, n)), slab_out],
        out_shape=[jax.ShapeDtypeStruct((T, D), _BF16), slab_shape],
        compiler_params=_params("arbitrary", "arbitrary"),
        name="branch_mix",
    )(y_pool, y_sb, y_hg, act, act, act, w_pool, w_sb, w_hg, w_out)


def _out_proj_kernel(rows, x_ref, m_ref, w_ref, g_ref, o_ref, hn_ref):
    w = w_ref[...]
    for r0 in range(0, x_ref.shape[0], rows):
        rs = slice(r0, r0 + rows)
        out = x_ref[rs, :] + _dot(m_ref[rs, :], w)
        o_ref[rs, :] = out
        hn_ref[rs, :] = _rms_rows(out, g_ref[...]).astype(hn_ref.dtype)


def out_proj(x, mixed, w_out_bf, gains, layer, tm=512, rows=256):
    T, D = x.shape
    return pl.pallas_call(
        functools.partial(_out_proj_kernel, rows),
        grid=(T // tm,),
        in_specs=[pl.BlockSpec((tm, D), lambda i: (i, 0)),
                  pl.BlockSpec((tm, D), lambda i: (i, 0)),
                  pl.BlockSpec((D, D), lambda i: (0, 0), pipeline_mode=pl.Buffered(1)),
                  pl.BlockSpec((None, 1, D), _layer_row(layer))],
        out_specs=[pl.BlockSpec((tm, D), lambda i: (i, 0)), pl.BlockSpec((tm, D), lambda i: (i, 0))],
        out_shape=[jax.ShapeDtypeStruct((T, D), _F32), jax.ShapeDtypeStruct((T, D), _BF16)],
        compiler_params=_params("arbitrary"),
        name="out_proj",
    )(x, mixed, w_out_bf, gains.reshape(-1, 1, D))


def _cast_slab_specs(w, layer, n_steps, step_of):
    rows = w.shape[1] // n_steps
    assert rows * n_steps == w.shape[1] and rows % 16 == 0
    in_spec = pl.BlockSpec((None, rows, w.shape[2]), lambda *idx: (layer, step_of(*idx), 0))
    out_spec = pl.BlockSpec((rows, w.shape[2]), lambda *idx: (step_of(*idx), 0))
    return in_spec, out_spec, jax.ShapeDtypeStruct(w.shape[1:], _BF16)


def _ffn_up_kernel(h_ref, wg_ref, wu_ref, slab_ref, o_ref, slab_bf_ref):
    h = h_ref[...]
    gate = _dot(h, wg_ref[...].astype(_BF16))
    up = _dot(h, wu_ref[...].astype(_BF16))
    o_ref[...] = (_silu(gate) * up).astype(o_ref.dtype)
    slab_bf_ref[...] = slab_ref[...].astype(_BF16)


def ffn_up(h, w_gate_up, w_down, layer, tm=2048, tn=512):
    T, D = h.shape
    ni, nj = T // tm, D_FF // tn
    slab_in, slab_out, slab_shape = _cast_slab_specs(w_down, layer, ni * nj, lambda i, j: i * nj + j)
    return pl.pallas_call(
        _ffn_up_kernel,
        grid=(ni, nj),
        in_specs=[pl.BlockSpec((tm, D), lambda i, j: (i, 0)),
                  pl.BlockSpec((None, D, tn), lambda i, j: (layer, 0, j)),
                  pl.BlockSpec((None, D, tn), lambda i, j: (layer, 0, nj + j)),
                  slab_in],
        out_specs=[pl.BlockSpec((tm, tn), lambda i, j: (i, j)), slab_out],
        out_shape=[jax.ShapeDtypeStruct((T, D_FF), _BF16), slab_shape],
        compiler_params=_params("arbitrary", "arbitrary"),
        name="ffn_up",
    )(h, w_gate_up, w_gate_up, w_down)


def _ffn_down_kernel(x_ref, a_ref, w_ref, slab_ref, o_ref, slab_bf_ref):
    o_ref[...] = x_ref[...] + _dot(a_ref[...], w_ref[...])
    slab_bf_ref[...] = slab_ref[...].astype(_BF16)


def ffn_down(x, hidden, w_down_bf, w_ple_gate, layer, tm=1024, tn=512):
    T, D = x.shape
    ni, nj = T // tm, D // tn
    slab_in, slab_out, slab_shape = _cast_slab_specs(w_ple_gate, layer, ni * nj, lambda i, j: i * nj + j)
    return pl.pallas_call(
        _ffn_down_kernel,
        grid=(ni, nj),
        in_specs=[pl.BlockSpec((tm, tn), lambda i, j: (i, j)),
                  pl.BlockSpec((tm, D_FF), lambda i, j: (i, 0)),
                  pl.BlockSpec((D_FF, tn), lambda i, j: (0, j)),
                  slab_in],
        out_specs=[pl.BlockSpec((tm, tn), lambda i, j: (i, j)), slab_out],
        out_shape=[jax.ShapeDtypeStruct((T, D), _F32), slab_shape],
        compiler_params=_params("arbitrary", "arbitrary"),
        name="ffn_down",
    )(x, hidden, w_down_bf, w_ple_gate)


def _ple_kernel(tn, finish, x_ref, g_ref, p_ref, wg_ref, wp_ref, gf_ref, o_ref, hn_ref):
    j = pl.program_id(1)

    @pl.when(j == 0)
    def _():
        hn_ref[...] = _rms_rows(x_ref[...], g_ref[...]).astype(_BF16)

    gate = jax.nn.sigmoid(_dot(hn_ref[...], wg_ref[...]))
    emb = _dot(p_ref[...].astype(_BF16), wp_ref[...].astype(_BF16))
    cols = pl.ds(pl.multiple_of(j * tn, tn), tn)
    o_ref[:, cols] = x_ref[:, cols] + gate * emb

    if finish:
        @pl.when(j == pl.num_programs(1) - 1)
        def _():
            o_ref[...] = _rms_rows(o_ref[...], gf_ref[...])


def ple_gate(x, gains, p, w_gate_bf, w_proj, final_gain, layer, finish, tm=1024, tn=1024):
    T, D = x.shape
    return pl.pallas_call(
        functools.partial(_ple_kernel, tn, finish),
        grid=(T // tm, D // tn),
        in_specs=[pl.BlockSpec((tm, D), lambda i, j: (i, 0)),
                  pl.BlockSpec((None, 1, D), _layer_row(layer)),
                  pl.BlockSpec((None, tm, PLE_DIM), lambda i, j: (layer, i, 0)),
                  pl.BlockSpec((D, tn), lambda i, j: (0, j)),
                  pl.BlockSpec((None, PLE_DIM, tn), lambda i, j: (layer, 0, j)),
                  pl.BlockSpec((1, D), lambda i, j: (0, 0))],
        out_specs=pl.BlockSpec((tm, D), lambda i, j: (i, 0)),
        out_shape=jax.ShapeDtypeStruct((T, D), _F32),
        scratch_shapes=[pltpu.VMEM((tm, D), _BF16)],
        compiler_params=_params("arbitrary", "arbitrary"),
        name="ple_gate",
    )(x, gains.reshape(-1, 1, D), p.reshape(-1, T, PLE_DIM), w_gate_bf, w_proj, final_gain.reshape(1, D))


def kernel(x, p, norm_mix, w_in, pool_w, pool_scale, hg_lb, hg_norm, w_br_pool, w_br_sb, w_br_hg,
           w_out, norm_ffn, w_gate_up, w_down, norm_ple, w_ple_gate, w_ple_proj, norm_final):
    B, S, D = x.shape
    T = B * S
    depth = w_in.shape[0]
    xf = x.reshape(T, D)
    for i in range(depth):
        act, rest = norm_in_proj(xf, norm_mix, w_in, i)
        rest3 = rest.reshape(B, S, -1)
        y_pool = pool_mixer(rest3, pool_w[i].astype(_BF16), pool_scale[i])
        y_sb = stick_breaking(act.reshape(B, S, -1))
        y_hg = hgrn2_mixer(rest3, hg_lb, hg_norm, i)
        mixed, w_out_bf = branch_mix(act, y_pool.reshape(T, POOL_WIDTH), y_sb.reshape(T, SB_WIDTH),
                                     y_hg.reshape(T, HG_WIDTH), w_br_pool, w_br_sb, w_br_hg, w_out, i)
        xf, h_ffn = out_proj(xf, mixed, w_out_bf, norm_ffn, i)
        hidden, w_down_bf = ffn_up(h_ffn, w_gate_up, w_down, i)
        xf, w_ple_gate_bf = ffn_down(xf, hidden, w_down_bf, w_ple_gate, i)
        xf = ple_gate(xf, norm_ple, p, w_ple_gate_bf, w_ple_proj, norm_final, i, finish=(i == depth - 1))
    return xf.reshape(B, S, D)
```

```python
import functools

import jax
import jax.numpy as jnp
from jax import lax
from jax.experimental import pallas as pl
from jax.experimental.pallas import tpu as pltpu

D_MODEL = 2048
POOL_WIDTH = 512
POOL_WINDOWS = (2, 4, 8, 16)
POOL_GROUP_DIM = 128
SB_WIDTH = 1024
SB_HEAD_DIM = 128
SB_HEADS = 8
HG_WIDTH = 512
HG_HEAD_DIM = 128
HG_HEADS = 4
LB_FLOOR = 1e-20
D_FF = 5632
PLE_DIM = 256
EPS = 1e-6
IN_COLS = POOL_WIDTH + 3 * SB_WIDTH + 4 * HG_WIDTH + 3 * D_MODEL

LANE = 128

N_BRANCH = 3
QKV_WIDTH = 3 * SB_WIDTH
ACT_WIDTH = QKV_WIDTH + N_BRANCH * D_MODEL
REST_WIDTH = IN_COLS - ACT_WIDTH
COL_POOL = 0
COL_ZF = POOL_WIDTH // LANE
COL_HV = COL_ZF + HG_WIDTH // LANE
COL_HQ = COL_HV + HG_WIDTH // LANE
COL_OG = COL_HQ + HG_WIDTH // LANE

ATT_BLOCK = 128
ATT_GROUP = 3
ATT_HEADS = 8
_LOG2E = 1.4426950408889634
ATT_LOG_ZERO = -105.0
HG_CHUNK = 128
HG_BAND = 4
HG_PAD = 8
VMEM_LIMIT = 56 * 1024 * 1024
IN_PROJ_TILE = 512

_F32 = jnp.float32
_BF16 = jnp.bfloat16


def _params(*sem):
    return pltpu.CompilerParams(dimension_semantics=sem, vmem_limit_bytes=VMEM_LIMIT)


def _dot(a, b):
    return jnp.dot(a, b, preferred_element_type=_F32)


def _dot_nt(a, b):
    return lax.dot_general(a, b, (((1,), (1,)), ((), ())), preferred_element_type=_F32)


def _dot_tn(a, b):
    return lax.dot_general(a, b, (((0,), (0,)), ((), ())), preferred_element_type=_F32)


def _split_dot(lhs_fn, x):
    hi = x.astype(_BF16)
    lo = (x - hi.astype(_F32)).astype(_BF16)
    return lhs_fn(hi) + lhs_fn(lo)


def _rms_rows(x, g):
    return x * lax.rsqrt(jnp.mean(x * x, axis=-1, keepdims=True) + EPS) * g


def _silu(x):
    return x * jax.nn.sigmoid(x)


def _layer_row(layer):
    return lambda *_: (layer, 0, 0)


def _cast_slab_specs(w, layer, n_steps, step_of):
    rows = w.shape[1] // n_steps
    assert rows * n_steps == w.shape[1] and rows % 16 == 0
    in_spec = pl.BlockSpec((None, rows, w.shape[2]), lambda *idx: (layer, step_of(*idx), 0))
    out_spec = pl.BlockSpec((rows, w.shape[2]), lambda *idx: (step_of(*idx), 0))
    return in_spec, out_spec, jax.ShapeDtypeStruct(w.shape[1:], _BF16)


def _in_proj_kernel(x_ref, g_ref, w_ref, act_ref, rest_ref, hn_ref):
    @pl.when(pl.program_id(1) == 0)
    def _():
        hn_ref[...] = _rms_rows(x_ref[...], g_ref[...]).astype(_BF16)

    r = _dot(hn_ref[...], w_ref[...].astype(_BF16))
    act_ref[...] = r.astype(_BF16)
    rest_ref[...] = r


def norm_in_proj(x, gains, w, layer, tm=2048, tn=IN_PROJ_TILE):
    T, D = x.shape
    n_pool, n_qkv, n_hg, n_gate = POOL_WIDTH // tn, QKV_WIDTH // tn, 4 * HG_WIDTH // tn, N_BRANCH * D // tn
    n_act, n_rest = n_qkv + n_gate, n_pool + n_hg

    def w_tile(i, j):
        tile = jnp.where(j < n_qkv, j + n_pool,
                         jnp.where(j < n_act, j + n_pool + n_hg,
                                   jnp.where(j < n_act + n_pool, j - n_act, j - n_gate)))
        return layer, 0, tile

    def act_tile(i, j):
        return i, jnp.minimum(j, n_act)

    def rest_tile(i, j):
        return i, jnp.where(j < n_act, n_rest, j - n_act)

    return pl.pallas_call(
        _in_proj_kernel,
        grid=(T // tm, n_act + n_rest),
        in_specs=[pl.BlockSpec((tm, D), lambda i, j: (i, 0), pipeline_mode=pl.Buffered(1)),
                  pl.BlockSpec((None, 1, D), _layer_row(layer)),
                  pl.BlockSpec((None, D, tn), w_tile)],
        out_specs=[pl.BlockSpec((tm, tn), act_tile), pl.BlockSpec((tm, tn), rest_tile)],
        out_shape=[jax.ShapeDtypeStruct((T, ACT_WIDTH + tn), _BF16),
                   jax.ShapeDtypeStruct((T, REST_WIDTH + tn), _F32)],
        scratch_shapes=[pltpu.VMEM((tm, D), _BF16)],
        compiler_params=_params("arbitrary", "arbitrary"),
        name="norm_in_proj",
    )(x, gains.reshape(-1, 1, D), w)


def _pool_kernel(u_ref, w_ref, s_ref, o_ref):
    S = u_ref.shape[0]
    t = lax.broadcasted_iota(jnp.int32, (S, POOL_GROUP_DIM), 0)
    for gi, win in enumerate(POOL_WINDOWS):
        cols = slice(gi * POOL_GROUP_DIM, (gi + 1) * POOL_GROUP_DIM)
        u = u_ref[:, cols]
        acc = u
        shift = 1
        while shift < win:
            acc = acc + jnp.where(t >= shift, pltpu.roll(acc, shift, axis=0), 0.0)
            shift *= 2
        cnt = jnp.minimum(t + 1, win).astype(_F32)
        mixed = (acc / cnt - u).astype(_BF16)
        y = _dot(mixed, w_ref[gi]) * s_ref[:, cols]
        o_ref[:, cols] = y.astype(o_ref.dtype)


def pool_mixer(proj3, pool_w, pool_scale):
    B, S, _ = proj3.shape
    return pl.pallas_call(
        _pool_kernel,
        grid=(B,),
        in_specs=[pl.BlockSpec((None, S, POOL_WIDTH), lambda b: (b, 0, COL_POOL)),
                  pl.BlockSpec((len(POOL_WINDOWS), POOL_GROUP_DIM, POOL_GROUP_DIM), lambda b: (0, 0, 0)),
                  pl.BlockSpec((1, POOL_WIDTH), lambda b: (0, 0))],
        out_specs=pl.BlockSpec((None, S, POOL_WIDTH), lambda b: (b, 0, 0)),
        out_shape=jax.ShapeDtypeStruct((B, S, POOL_WIDTH), _BF16),
        compiler_params=_params("arbitrary"),
        name="pool_mixer",
    )(proj3, pool_w, pool_scale.reshape(1, POOL_WIDTH))


def _sb_kernel(q_ref, k_ref, v_ref, o_ref):
    i = pl.program_id(2)
    blk, group, heads = ATT_BLOCK, ATT_GROUP, ATT_HEADS
    scale = SB_HEAD_DIM ** -0.5
    q_all = q_ref[...]
    key_minus_query = (lax.broadcasted_iota(jnp.int32, (blk, blk), 1)
                       - lax.broadcasted_iota(jnp.int32, (blk, blk), 0))
    r2 = lax.broadcasted_iota(jnp.int32, (2 * blk, 2 * blk), 0) % blk
    c2 = lax.broadcasted_iota(jnp.int32, (2 * blk, 2 * blk), 1)
    neg_sums = jnp.where((r2 >= c2) | (c2 >= blk), -1.0, 0.0).astype(_BF16)

    def body(carry):
        step, _, accs, laters = carry
        diag_mask = key_minus_query < jnp.where(step == 0, 0, blk)
        parts = []
        for hd in range(heads):
            cols = slice(hd * SB_HEAD_DIM, (hd + 1) * SB_HEAD_DIM)
            q = q_all[:, cols]
            for u in range(group):
                j = i - group * step - u
                start = pl.multiple_of(jnp.maximum(j, 0) * blk, blk)
                d = _dot_nt(q, k_ref[pl.ds(start, blk), cols])
                z = d * scale
                softplus = jnp.maximum(z, 0.0) + jnp.log(1.0 + jnp.exp2(jnp.abs(d) * (-scale * _LOG2E)))
                if u == 0:
                    softplus = jnp.where(diag_mask, softplus, 0.0)
                hi = softplus.astype(_BF16)
                lo = (softplus - hi.astype(_F32)).astype(_BF16)
                sums = _dot(jnp.concatenate([hi, lo], axis=1), neg_sums)
                parts.append((hd, u, j, start, z, sums))
        accs, laters = list(accs), list(laters)
        for hd, u, j, start, z, sums in parts:
            cols = slice(hd * SB_HEAD_DIM, (hd + 1) * SB_HEAD_DIM)
            a = jnp.exp(z + (sums[:, :blk] + laters[hd]))
            vb = v_ref[pl.ds(start, blk), cols]
            if u == 0:
                a = jnp.where(diag_mask, a, 0.0)
            else:
                vb = jnp.where(j >= 0, vb, jnp.zeros_like(vb))
            accs[hd] = accs[hd] + _dot(a.astype(_BF16), vb)
            laters[hd] = laters[hd] + sums[:, blk:]
        largest_later = jnp.max(functools.reduce(jnp.maximum, laters))
        return step + 1, largest_later, tuple(accs), tuple(laters)

    def unfinished(carry):
        step, largest_later, _, _ = carry
        return (step <= i // group) & (largest_later > ATT_LOG_ZERO)

    zeros = tuple(jnp.zeros((blk, blk), _F32) for _ in range(heads))
    _, _, accs, _ = lax.while_loop(unfinished, body, (jnp.int32(0), jnp.float32(0.0), zeros, zeros))
    o_ref[...] = jnp.concatenate(accs, axis=1).astype(o_ref.dtype)


def stick_breaking(qkv3):
    B, S, _ = qkv3.shape
    blk, heads = ATT_BLOCK, ATT_HEADS
    width = heads * SB_HEAD_DIM
    per = SB_WIDTH // width
    return pl.pallas_call(
        _sb_kernel,
        grid=(B, per, S // blk),
        in_specs=[pl.BlockSpec((None, blk, width), lambda b, h, i: (b, i, h)),
                  pl.BlockSpec((None, S, width), lambda b, h, i: (b, 0, per + h)),
                  pl.BlockSpec((None, S, width), lambda b, h, i: (b, 0, 2 * per + h))],
        out_specs=pl.BlockSpec((None, blk, width), lambda b, h, i: (b, i, h)),
        out_shape=jax.ShapeDtypeStruct((B, S, SB_WIDTH), _BF16),
        compiler_params=_params("arbitrary", "arbitrary", "arbitrary"),
        name="stick_breaking",
    )(qkv3, qkv3, qkv3)


def _hg_kernel(layer, zf_ref, hv_ref, hq_ref, og_ref, lb_ref, gn_ref, o_ref,
               state_ref, kpad_ref, bpad_ref, vpad_ref):
    C, W, pad = HG_CHUNK, HG_HEAD_DIM, HG_PAD
    heads = [slice(h * W, (h + 1) * W) for h in range(HG_HEADS)]

    @pl.when(pl.program_id(1) == 0)
    def _():
        state_ref[...] = jnp.zeros_like(state_ref)

    lbs = [lb_ref[d] for d in range(lb_ref.shape[0])]
    top = functools.reduce(jnp.maximum, lbs)
    es = [jnp.exp(row - top) for row in lbs]
    total = functools.reduce(jnp.add, es)
    sm = [e / total for e in es]
    lb = jnp.clip(functools.reduce(jnp.add, sm[:layer + 1]) - sm[0], 0.0, 1.0)

    z = zf_ref[...]
    t = jnp.exp(-jnp.abs(z))
    big = 1.0 / (1.0 + t)
    small = t * big
    sig_pos = jnp.where(z >= 0.0, big, small)
    sig_neg = jnp.where(z >= 0.0, small, big)
    log_f = jnp.log(jnp.maximum(lb, LB_FLOOR) + (1.0 - jnp.minimum(lb, 1.0 - 1e-6)) * sig_pos)
    k = (1.0 - lb) * sig_neg
    q = _silu(hq_ref[...])
    v = hv_ref[...]
    v_bf = v.astype(_BF16)

    r = lax.broadcasted_iota(jnp.int32, (C, C), 0)
    c = lax.broadcasted_iota(jnp.int32, (C, C), 1)
    tri = jnp.where(c <= r, 1.0, 0.0).astype(_BF16)
    b = _split_dot(lambda part: _dot(tri, part), log_f) * _LOG2E

    states = [state_ref[h] for h in range(HG_HEADS)]
    q_dec = (q * jnp.exp2(b)).astype(_BF16)
    outs = [_dot_nt(q_dec[:, hs], states[h].astype(_BF16)) for h, hs in enumerate(heads)]

    scores = [jnp.zeros((C, C), _F32) for _ in heads]
    m = C // 2
    while m >= HG_BAND:
        span = 2 * m
        ref = jnp.concatenate(
            [jnp.broadcast_to(b[g * span + m - 1:g * span + m, :], (span, HG_WIDTH)) for g in range(C // span)],
            axis=0)
        qt = (q * jnp.exp2(jnp.minimum(b - ref, 0.0))).astype(_BF16)
        kt = (k * jnp.exp2(jnp.minimum(ref - b, 0.0))).astype(_BF16)
        sel = ((r // span) == (c // span)) & ((r % span) >= m) & ((c % span) < m)
        scores = [jnp.where(sel, _dot_nt(qt[:, hs], kt[:, hs]), scores[h]) for h, hs in enumerate(heads)]
        m //= 2
    outs = [outs[h] + _dot(scores[h].astype(_BF16), v_bf[:, hs]) for h, hs in enumerate(heads)]

    zeros_pad = jnp.zeros((pad, W), _F32)
    for ref_, val in ((kpad_ref, k), (bpad_ref, b), (vpad_ref, v)):
        for h, hs in enumerate(heads):
            ref_[h, 0:pad, :] = zeros_pad
            ref_[h, pad:pad + C, :] = val[:, hs]
    sub = lax.broadcasted_iota(jnp.int32, (C, 1), 0) % HG_BAND
    for d in range(HG_BAND):
        rows = slice(pad - d, pad - d + C)
        for h, hs in enumerate(heads):
            decay = jnp.exp2(jnp.minimum(b[:, hs] - bpad_ref[h, rows, :], 0.0))
            w = jnp.sum(q[:, hs] * kpad_ref[h, rows, :] * decay, axis=-1, keepdims=True)
            outs[h] = outs[h] + jnp.where(sub >= d, w, 0.0) * vpad_ref[h, rows, :]

    b_last = b[C - 1:C, :]
    k_dec = (k * jnp.exp2(b_last - b)).astype(_BF16)
    keep = jnp.exp2(b_last)
    for h, hs in enumerate(heads):
        state_ref[h] = states[h] * keep[:, hs] + _dot_tn(v_bf[:, hs], k_dec[:, hs])

    normed = [o * lax.rsqrt(jnp.mean(o * o, axis=-1, keepdims=True) + EPS) for o in outs]
    o_ref[...] = (jnp.concatenate(normed, axis=1) * gn_ref[...] * _silu(og_ref[...])).astype(o_ref.dtype)


def hgrn2_mixer(rest3, hg_lb, hg_norm, layer):
    B, S, _ = rest3.shape
    depth = hg_lb.shape[0]
    C, W = HG_CHUNK, HG_HEAD_DIM

    def cols(base):
        return pl.BlockSpec((None, C, HG_WIDTH), lambda b, c: (b, c, base * LANE // HG_WIDTH))

    return pl.pallas_call(
        functools.partial(_hg_kernel, layer),
        grid=(B, S // C),
        in_specs=[cols(COL_ZF), cols(COL_HV), cols(COL_HQ), cols(COL_OG),
                  pl.BlockSpec((depth, 1, HG_WIDTH), lambda b, c: (0, 0, 0)),
                  pl.BlockSpec((None, 1, HG_WIDTH), lambda b, c: (layer, 0, 0))],
        out_specs=pl.BlockSpec((None, C, HG_WIDTH), lambda b, c: (b, c, 0)),
        out_shape=jax.ShapeDtypeStruct((B, S, HG_WIDTH), _BF16),
        scratch_shapes=[pltpu.VMEM((HG_HEADS, W, W), _F32)] + [pltpu.VMEM((HG_HEADS, HG_PAD + C, W), _F32)] * 3,
        compiler_params=_params("arbitrary", "arbitrary"),
        name="hgrn2_mixer",
    )(rest3, rest3, rest3, rest3, hg_lb.reshape(depth, 1, HG_WIDTH), hg_norm.reshape(depth, 1, HG_WIDTH))


def _branch_mix_kernel(yp_ref, ys_ref, yh_ref, g0_ref, g1_ref, g2_ref, wp_ref, ws_ref, wh_ref, slab_ref,
                       o_ref, slab_bf_ref, wp_bf, ws_bf, wh_bf):
    @pl.when(pl.program_id(1) == 0)
    def _():
        wp_bf[...] = wp_ref[...].astype(_BF16)
        ws_bf[...] = ws_ref[...].astype(_BF16)
        wh_bf[...] = wh_ref[...].astype(_BF16)

    def gate(g_ref):
        return jax.nn.sigmoid(g_ref[...].astype(_F32))

    mixed = (gate(g0_ref) * _dot(yp_ref[...], wp_bf[...])
             + gate(g1_ref) * _dot(ys_ref[...], ws_bf[...])
             + gate(g2_ref) * _dot(yh_ref[...], wh_bf[...]))
    o_ref[...] = mixed.astype(o_ref.dtype)
    slab_bf_ref[...] = slab_ref[...].astype(_BF16)


def branch_mix(act, y_pool, y_sb, y_hg, w_pool, w_sb, w_hg, w_out, layer, tm=1024, tn=1024):
    T = act.shape[0]
    D = D_MODEL
    gate0 = QKV_WIDTH // tn
    per_gate = D // tn
    n_tok = T // tm
    slab_in, slab_out, slab_shape = _cast_slab_specs(w_out, layer, per_gate * n_tok, lambda n, i: n * n_tok + i)

    def gate(idx):
        return pl.BlockSpec((tm, tn), lambda n, i: (i, gate0 + idx * per_gate + n))

    def weight(width):
        return pl.BlockSpec((None, width, tn), lambda n, i: (layer, 0, n), pipeline_mode=pl.Buffered(1))

    return pl.pallas_call(
        _branch_mix_kernel,
        grid=(per_gate, n_tok),
        in_specs=[pl.BlockSpec((tm, POOL_WIDTH), lambda n, i: (i, 0)),
                  pl.BlockSpec((tm, SB_WIDTH), lambda n, i: (i, 0)),
                  pl.BlockSpec((tm, HG_WIDTH), lambda n, i: (i, 0)),
                  gate(0), gate(1), gate(2),
                  weight(POOL_WIDTH), weight(SB_WIDTH), weight(HG_WIDTH), slab_in],
        out_specs=[pl.BlockSpec((tm, tn), lambda n, i: (i, n)), slab_out],
        out_shape=[jax.ShapeDtypeStruct((T, D), _BF16), slab_shape],
        scratch_shapes=[pltpu.VMEM((POOL_WIDTH, tn), _BF16), pltpu.VMEM((SB_WIDTH, tn), _BF16),
                        pltpu.VMEM((HG_WIDTH, tn), _BF16)],
        compiler_params=_params("arbitrary", "arbitrary"),
        name="branch_mix",
    )(y_pool, y_sb, y_hg, act, act, act, w_pool, w_sb, w_hg, w_out)


def _out_proj_kernel(rows, x_ref, m_ref, w_ref, g_ref, o_ref, hn_ref):
    w = w_ref[...]
    for r0 in range(0, x_ref.shape[0], rows):
        rs = slice(r0, r0 + rows)
        out = x_ref[rs, :] + _dot(m_ref[rs, :], w)
        o_ref[rs, :] = out
        hn_ref[rs, :] = _rms_rows(out, g_ref[...]).astype(hn_ref.dtype)


def out_proj(x, mixed, w_out_bf, gains, layer, tm=512, rows=256):
    T, D = x.shape
    return pl.pallas_call(
        functools.partial(_out_proj_kernel, rows),
        grid=(T // tm,),
        in_specs=[pl.BlockSpec((tm, D), lambda i: (i, 0)),
                  pl.BlockSpec((tm, D), lambda i: (i, 0)),
                  pl.BlockSpec((D, D), lambda i: (0, 0), pipeline_mode=pl.Buffered(1)),
                  pl.BlockSpec((None, 1, D), _layer_row(layer))],
        out_specs=[pl.BlockSpec((tm, D), lambda i: (i, 0)), pl.BlockSpec((tm, D), lambda i: (i, 0))],
        out_shape=[jax.ShapeDtypeStruct((T, D), _F32), jax.ShapeDtypeStruct((T, D), _BF16)],
        compiler_params=_params("arbitrary"),
        name="out_proj",
    )(x, mixed, w_out_bf, gains.reshape(-1, 1, D))


def _ffn_up_kernel(h_ref, wg_ref, wu_ref, slab_ref, o_ref, slab_bf_ref):
    h = h_ref[...]
    gate = _dot(h, wg_ref[...].astype(_BF16))
    up = _dot(h, wu_ref[...].astype(_BF16))
    o_ref[...] = (_silu(gate) * up).astype(o_ref.dtype)
    slab_bf_ref[...] = slab_ref[...].astype(_BF16)


def ffn_up(h, w_gate_up, w_down, layer, tm=2048, tn=512):
    T, D = h.shape
    ni, nj = T // tm, D_FF // tn
    slab_in, slab_out, slab_shape = _cast_slab_specs(w_down, layer, ni * nj, lambda i, j: i * nj + j)
    return pl.pallas_call(
        _ffn_up_kernel,
        grid=(ni, nj),
        in_specs=[pl.BlockSpec((tm, D), lambda i, j: (i, 0)),
                  pl.BlockSpec((None, D, tn), lambda i, j: (layer, 0, j)),
                  pl.BlockSpec((None, D, tn), lambda i, j: (layer, 0, nj + j)),
                  slab_in],
        out_specs=[pl.BlockSpec((tm, tn), lambda i, j: (i, j)), slab_out],
        out_shape=[jax.ShapeDtypeStruct((T, D_FF), _BF16), slab_shape],
        compiler_params=_params("arbitrary", "arbitrary"),
        name="ffn_up",
    )(h, w_gate_up, w_gate_up, w_down)


def _ffn_down_kernel(x_ref, a_ref, w_ref, slab_ref, o_ref, slab_bf_ref):
    o_ref[...] = x_ref[...] + _dot(a_ref[...], w_ref[...])
    slab_bf_ref[...] = slab_ref[...].astype(_BF16)


def ffn_down(x, hidden, w_down_bf, w_ple_gate, layer, tm=1024, tn=512):
    T, D = x.shape
    ni, nj = T // tm, D // tn
    slab_in, slab_out, slab_shape = _cast_slab_specs(w_ple_gate, layer, ni * nj, lambda i, j: i * nj + j)
    return pl.pallas_call(
        _ffn_down_kernel,
        grid=(ni, nj),
        in_specs=[pl.BlockSpec((tm, tn), lambda i, j: (i, j)),
                  pl.BlockSpec((tm, D_FF), lambda i, j: (i, 0)),
                  pl.BlockSpec((D_FF, tn), lambda i, j: (0, j)),
                  slab_in],
        out_specs=[pl.BlockSpec((tm, tn), lambda i, j: (i, j)), slab_out],
        out_shape=[jax.ShapeDtypeStruct((T, D), _F32), slab_shape],
        compiler_params=_params("arbitrary", "arbitrary"),
        name="ffn_down",
    )(x, hidden, w_down_bf, w_ple_gate)


def _ple_kernel(tn, finish, x_ref, g_ref, p_ref, wg_ref, wp_ref, gf_ref, o_ref, hn_ref):
    j = pl.program_id(1)

    @pl.when(j == 0)
    def _():
        hn_ref[...] = _rms_rows(x_ref[...], g_ref[...]).astype(_BF16)

    gate = jax.nn.sigmoid(_dot(hn_ref[...], wg_ref[...]))
    emb = _dot(p_ref[...].astype(_BF16), wp_ref[...].astype(_BF16))
    cols = pl.ds(pl.multiple_of(j * tn, tn), tn)
    o_ref[:, cols] = x_ref[:, cols] + gate * emb

    if finish:
        @pl.when(j == pl.num_programs(1) - 1)
        def _():
            o_ref[...] = _rms_rows(o_ref[...], gf_ref[...])


def ple_gate(x, gains, p, w_gate_bf, w_proj, final_gain, layer, finish, tm=1024, tn=1024):
    T, D = x.shape
    return pl.pallas_call(
        functools.partial(_ple_kernel, tn, finish),
        grid=(T // tm, D // tn),
        in_specs=[pl.BlockSpec((tm, D), lambda i, j: (i, 0)),
                  pl.BlockSpec((None, 1, D), _layer_row(layer)),
                  pl.BlockSpec((None, tm, PLE_DIM), lambda i, j: (layer, i, 0)),
                  pl.BlockSpec((D, tn), lambda i, j: (0, j)),
                  pl.BlockSpec((None, PLE_DIM, tn), lambda i, j: (layer, 0, j)),
                  pl.BlockSpec((1, D), lambda i, j: (0, 0))],
        out_specs=pl.BlockSpec((tm, D), lambda i, j: (i, 0)),
        out_shape=jax.ShapeDtypeStruct((T, D), _F32),
        scratch_shapes=[pltpu.VMEM((tm, D), _BF16)],
        compiler_params=_params("arbitrary", "arbitrary"),
        name="ple_gate",
    )(x, gains.reshape(-1, 1, D), p.reshape(-1, T, PLE_DIM), w_gate_bf, w_proj, final_gain.reshape(1, D))


def kernel(x, p, norm_mix, w_in, pool_w, pool_scale, hg_lb, hg_norm, w_br_pool, w_br_sb, w_br_hg,
           w_out, norm_ffn, w_gate_up, w_down, norm_ple, w_ple_gate, w_ple_proj, norm_final):
    B, S, D = x.shape
    T = B * S
    depth = w_in.shape[0]
    xf = x.reshape(T, D)
    for i in range(depth):
        act, rest = norm_in_proj(xf, norm_mix, w_in, i)
        rest3 = rest.reshape(B, S, -1)
        y_pool = pool_mixer(rest3, pool_w[i].astype(_BF16), pool_scale[i])
        y_sb = stick_breaking(act.reshape(B, S, -1))
        y_hg = hgrn2_mixer(rest3, hg_lb, hg_norm, i)
        mixed, w_out_bf = branch_mix(act, y_pool.reshape(T, POOL_WIDTH), y_sb.reshape(T, SB_WIDTH),
                                     y_hg.reshape(T, HG_WIDTH), w_br_pool, w_br_sb, w_br_hg, w_out, i)
        xf, h_ffn = out_proj(xf, mixed, w_out_bf, norm_ffn, i)
        hidden, w_down_bf = ffn_up(h_ffn, w_gate_up, w_down, i)
        xf, w_ple_gate_bf = ffn_down(xf, hidden, w_down_bf, w_ple_gate, i)
        xf = ple_gate(xf, norm_ple, p, w_ple_gate_bf, w_ple_proj, norm_final, i, finish=(i == depth - 1))
    return xf.reshape(B, S, D)
```

```python
import functools

import jax
import jax.numpy as jnp
from jax import lax
from jax.experimental import pallas as pl
from jax.experimental.pallas import tpu as pltpu

D_MODEL = 2048
POOL_WIDTH = 512
POOL_WINDOWS = (2, 4, 8, 16)
POOL_GROUP_DIM = 128
SB_WIDTH = 1024
SB_HEAD_DIM = 128
SB_HEADS = 8
HG_WIDTH = 512
HG_HEAD_DIM = 128
HG_HEADS = 4
LB_FLOOR = 1e-20
D_FF = 5632
PLE_DIM = 256
EPS = 1e-6
IN_COLS = POOL_WIDTH + 3 * SB_WIDTH + 4 * HG_WIDTH + 3 * D_MODEL

LANE = 128

N_BRANCH = 3
QKV_WIDTH = 3 * SB_WIDTH
ACT_WIDTH = QKV_WIDTH + N_BRANCH * D_MODEL
REST_WIDTH = IN_COLS - ACT_WIDTH
COL_POOL = 0
COL_ZF = POOL_WIDTH // LANE
COL_HV = COL_ZF + HG_WIDTH // LANE
COL_HQ = COL_HV + HG_WIDTH // LANE
COL_OG = COL_HQ + HG_WIDTH // LANE

ATT_BLOCK = 128
ATT_GROUP = 3
ATT_HEADS = 8
ATT_QBLOCKS = 4
_LOG2E = 1.4426950408889634
ATT_LOG_ZERO = -105.0
HG_CHUNK = 128
HG_STEP_CHUNKS = 4
HG_BAND = 4
HG_PAD = 8
VMEM_LIMIT = 56 * 1024 * 1024
IN_PROJ_TILE = 512

_F32 = jnp.float32
_BF16 = jnp.bfloat16


def _params(*sem):
    return pltpu.CompilerParams(dimension_semantics=sem, vmem_limit_bytes=VMEM_LIMIT)


def _dot(a, b):
    return jnp.dot(a, b, preferred_element_type=_F32)


def _dot_nt(a, b):
    return lax.dot_general(a, b, (((1,), (1,)), ((), ())), preferred_element_type=_F32)


def _dot_tn(a, b):
    return lax.dot_general(a, b, (((0,), (0,)), ((), ())), preferred_element_type=_F32)


def _split_dot(lhs_fn, x):
    hi = x.astype(_BF16)
    lo = (x - hi.astype(_F32)).astype(_BF16)
    return lhs_fn(hi) + lhs_fn(lo)


def _rms_rows(x, g):
    return x * lax.rsqrt(jnp.mean(x * x, axis=-1, keepdims=True) + EPS) * g


def _silu(x):
    return x * jax.nn.sigmoid(x)


def _layer_row(layer):
    return lambda *_: (layer, 0, 0)


def _cast_slab_specs(w, layer, n_steps, step_of):
    rows = w.shape[1] // n_steps
    assert rows * n_steps == w.shape[1] and rows % 16 == 0
    in_spec = pl.BlockSpec((None, rows, w.shape[2]), lambda *idx: (layer, step_of(*idx), 0))
    out_spec = pl.BlockSpec((rows, w.shape[2]), lambda *idx: (step_of(*idx), 0))
    return in_spec, out_spec, jax.ShapeDtypeStruct(w.shape[1:], _BF16)


def _in_proj_kernel(x_ref, g_ref, w_ref, act_ref, rest_ref, hn_ref):
    @pl.when(pl.program_id(1) == 0)
    def _():
        hn_ref[...] = _rms_rows(x_ref[...], g_ref[...]).astype(_BF16)

    r = _dot(hn_ref[...], w_ref[...].astype(_BF16))
    act_ref[...] = r.astype(_BF16)
    rest_ref[...] = r


def norm_in_proj(x, gains, w, layer, tm=2048, tn=IN_PROJ_TILE):
    T, D = x.shape
    n_pool, n_qkv, n_hg, n_gate = POOL_WIDTH // tn, QKV_WIDTH // tn, 4 * HG_WIDTH // tn, N_BRANCH * D // tn
    n_act, n_rest = n_qkv + n_gate, n_pool + n_hg

    def w_tile(i, j):
        tile = jnp.where(j < n_qkv, j + n_pool,
                         jnp.where(j < n_act, j + n_pool + n_hg,
                                   jnp.where(j < n_act + n_pool, j - n_act, j - n_gate)))
        return layer, 0, tile

    def act_tile(i, j):
        return i, jnp.minimum(j, n_act)

    def rest_tile(i, j):
        return i, jnp.where(j < n_act, n_rest, j - n_act)

    return pl.pallas_call(
        _in_proj_kernel,
        grid=(T // tm, n_act + n_rest),
        in_specs=[pl.BlockSpec((tm, D), lambda i, j: (i, 0), pipeline_mode=pl.Buffered(1)),
                  pl.BlockSpec((None, 1, D), _layer_row(layer)),
                  pl.BlockSpec((None, D, tn), w_tile)],
        out_specs=[pl.BlockSpec((tm, tn), act_tile), pl.BlockSpec((tm, tn), rest_tile)],
        out_shape=[jax.ShapeDtypeStruct((T, ACT_WIDTH + tn), _BF16),
                   jax.ShapeDtypeStruct((T, REST_WIDTH + tn), _F32)],
        scratch_shapes=[pltpu.VMEM((tm, D), _BF16)],
        compiler_params=_params("arbitrary", "arbitrary"),
        name="norm_in_proj",
    )(x, gains.reshape(-1, 1, D), w)


def _pool_kernel(u_ref, w_ref, s_ref, o_ref):
    S = u_ref.shape[0]
    t = lax.broadcasted_iota(jnp.int32, (S, POOL_GROUP_DIM), 0)
    for gi, win in enumerate(POOL_WINDOWS):
        cols = slice(gi * POOL_GROUP_DIM, (gi + 1) * POOL_GROUP_DIM)
        u = u_ref[:, cols]
        acc = u
        shift = 1
        while shift < win:
            acc = acc + jnp.where(t >= shift, pltpu.roll(acc, shift, axis=0), 0.0)
            shift *= 2
        cnt = jnp.minimum(t + 1, win).astype(_F32)
        mixed = (acc / cnt - u).astype(_BF16)
        y = _dot(mixed, w_ref[gi]) * s_ref[:, cols]
        o_ref[:, cols] = y.astype(o_ref.dtype)


def pool_mixer(proj3, pool_w, pool_scale):
    B, S, _ = proj3.shape
    return pl.pallas_call(
        _pool_kernel,
        grid=(B,),
        in_specs=[pl.BlockSpec((None, S, POOL_WIDTH), lambda b: (b, 0, COL_POOL)),
                  pl.BlockSpec((len(POOL_WINDOWS), POOL_GROUP_DIM, POOL_GROUP_DIM), lambda b: (0, 0, 0)),
                  pl.BlockSpec((1, POOL_WIDTH), lambda b: (0, 0))],
        out_specs=pl.BlockSpec((None, S, POOL_WIDTH), lambda b: (b, 0, 0)),
        out_shape=jax.ShapeDtypeStruct((B, S, POOL_WIDTH), _BF16),
        compiler_params=_params("arbitrary"),
        name="pool_mixer",
    )(proj3, pool_w, pool_scale.reshape(1, POOL_WIDTH))


def _sb_kernel(q_ref, k_ref, v_ref, o_ref):
    pair = pl.program_id(2)
    blk, group, heads, nq = ATT_BLOCK, ATT_GROUP, ATT_HEADS, ATT_QBLOCKS
    scale = SB_HEAD_DIM ** -0.5
    last_block = pair * nq + nq - 1
    q_all = q_ref[...]
    key_minus_query = (lax.broadcasted_iota(jnp.int32, (blk, blk), 1)
                       - lax.broadcasted_iota(jnp.int32, (blk, blk), 0))
    r2 = lax.broadcasted_iota(jnp.int32, (2 * blk, 2 * blk), 0) % blk
    c2 = lax.broadcasted_iota(jnp.int32, (2 * blk, 2 * blk), 1)
    neg_sums = jnp.where((r2 >= c2) | (c2 >= blk), -1.0, 0.0).astype(_BF16)

    def body(carry):
        step, _, accs, laters = carry
        diag_mask = key_minus_query < jnp.where(step == 0, 0, blk)
        parts = []
        for qb in range(nq):
            i = pair * nq + qb
            for hd in range(heads):
                cols = slice(hd * SB_HEAD_DIM, (hd + 1) * SB_HEAD_DIM)
                q = q_all[qb * blk:(qb + 1) * blk, cols]
                for u in range(group):
                    j = i - group * step - u
                    start = pl.multiple_of(jnp.maximum(j, 0) * blk, blk)
                    d = _dot_nt(q, k_ref[pl.ds(start, blk), cols])
                    z = d * scale
                    softplus = jnp.maximum(z, 0.0) + jnp.log(1.0 + jnp.exp2(jnp.abs(d) * (-scale * _LOG2E)))
                    if u == 0:
                        softplus = jnp.where(diag_mask, softplus, 0.0)
                    hi = softplus.astype(_BF16)
                    lo = (softplus - hi.astype(_F32)).astype(_BF16)
                    sums = _dot(jnp.concatenate([hi, lo], axis=1), neg_sums)
                    parts.append((qb * heads + hd, cols, u, j, start, z, sums))
        accs, laters = list(accs), list(laters)
        for slot, cols, u, j, start, z, sums in parts:
            a = jnp.exp(z + (sums[:, :blk] + laters[slot]))
            if u == 0:
                a = jnp.where(diag_mask, a, 0.0)
            vb = v_ref[pl.ds(start, blk), cols]
            vb = jnp.where(j >= 0, vb, jnp.zeros_like(vb))
            accs[slot] = accs[slot] + _dot(a.astype(_BF16), vb)
            laters[slot] = laters[slot] + sums[:, blk:]
        largest_later = jnp.max(functools.reduce(jnp.maximum, laters))
        return step + 1, largest_later, tuple(accs), tuple(laters)

    def unfinished(carry):
        step, largest_later, _, _ = carry
        return (step <= last_block // group) & (largest_later > ATT_LOG_ZERO)

    zeros = tuple(jnp.zeros((blk, blk), _F32) for _ in range(nq * heads))
    _, _, accs, _ = lax.while_loop(unfinished, body, (jnp.int32(0), jnp.float32(0.0), zeros, zeros))
    rows = [jnp.concatenate(accs[qb * heads:(qb + 1) * heads], axis=1) for qb in range(nq)]
    o_ref[...] = jnp.concatenate(rows, axis=0).astype(o_ref.dtype)


def stick_breaking(qkv3):
    B, S, _ = qkv3.shape
    heads = ATT_HEADS
    blk = ATT_BLOCK * ATT_QBLOCKS
    width = heads * SB_HEAD_DIM
    per = SB_WIDTH // width
    return pl.pallas_call(
        _sb_kernel,
        grid=(B, per, S // blk),
        in_specs=[pl.BlockSpec((None, blk, width), lambda b, h, i: (b, i, h)),
                  pl.BlockSpec((None, S, width), lambda b, h, i: (b, 0, per + h)),
                  pl.BlockSpec((None, S, width), lambda b, h, i: (b, 0, 2 * per + h))],
        out_specs=pl.BlockSpec((None, blk, width), lambda b, h, i: (b, i, h)),
        out_shape=jax.ShapeDtypeStruct((B, S, SB_WIDTH), _BF16),
        compiler_params=_params("arbitrary", "arbitrary", "arbitrary"),
        name="stick_breaking",
    )(qkv3, qkv3, qkv3)


def _hg_kernel(layer, zf_ref, hv_ref, hq_ref, og_ref, lb_ref, gn_ref, o_ref,
               state_ref, kpad_ref, bpad_ref, vpad_ref):
    C, W, pad = HG_CHUNK, HG_HEAD_DIM, HG_PAD
    heads = [slice(h * W, (h + 1) * W) for h in range(HG_HEADS)]

    @pl.when(pl.program_id(1) == 0)
    def _():
        state_ref[...] = jnp.zeros_like(state_ref)

    lbs = [lb_ref[d] for d in range(lb_ref.shape[0])]
    top = functools.reduce(jnp.maximum, lbs)
    es = [jnp.exp(row - top) for row in lbs]
    total = functools.reduce(jnp.add, es)
    sm = [e / total for e in es]
    lb = jnp.clip(functools.reduce(jnp.add, sm[:layer + 1]) - sm[0], 0.0, 1.0)

    r = lax.broadcasted_iota(jnp.int32, (C, C), 0)
    c = lax.broadcasted_iota(jnp.int32, (C, C), 1)
    tri = jnp.where(c <= r, 1.0, 0.0).astype(_BF16)
    sub = lax.broadcasted_iota(jnp.int32, (C, 1), 0) % HG_BAND
    zeros_pad = jnp.zeros((pad, W), _F32)

    def chunk(n, states):
        rows_n = slice(n * C, (n + 1) * C)
        z = zf_ref[rows_n, :]
        t = jnp.exp(-jnp.abs(z))
        big = 1.0 / (1.0 + t)
        small = t * big
        sig_pos = jnp.where(z >= 0.0, big, small)
        sig_neg = jnp.where(z >= 0.0, small, big)
        log_f = jnp.log(jnp.maximum(lb, LB_FLOOR) + (1.0 - jnp.minimum(lb, 1.0 - 1e-6)) * sig_pos)
        k = (1.0 - lb) * sig_neg
        q = _silu(hq_ref[rows_n, :])
        v = hv_ref[rows_n, :]
        v_bf = v.astype(_BF16)

        b = _split_dot(lambda part: _dot(tri, part), log_f) * _LOG2E

        q_dec = (q * jnp.exp2(b)).astype(_BF16)
        outs = [_dot_nt(q_dec[:, hs], states[h].astype(_BF16)) for h, hs in enumerate(heads)]

        scores = [jnp.zeros((C, C), _F32) for _ in heads]
        m = C // 2
        while m >= HG_BAND:
            span = 2 * m
            ref = jnp.concatenate(
                [jnp.broadcast_to(b[g * span + m - 1:g * span + m, :], (span, HG_WIDTH))
                 for g in range(C // span)], axis=0)
            qt = (q * jnp.exp2(jnp.minimum(b - ref, 0.0))).astype(_BF16)
            kt = (k * jnp.exp2(jnp.minimum(ref - b, 0.0))).astype(_BF16)
            sel = ((r // span) == (c // span)) & ((r % span) >= m) & ((c % span) < m)
            scores = [jnp.where(sel, _dot_nt(qt[:, hs], kt[:, hs]), scores[h]) for h, hs in enumerate(heads)]
            m //= 2
        outs = [outs[h] + _dot(scores[h].astype(_BF16), v_bf[:, hs]) for h, hs in enumerate(heads)]

        slots = [n * HG_HEADS + h for h in range(HG_HEADS)]
        for ref_, val in ((kpad_ref, k), (bpad_ref, b), (vpad_ref, v)):
            for h, hs in enumerate(heads):
                ref_[slots[h], 0:pad, :] = zeros_pad
                ref_[slots[h], pad:pad + C, :] = val[:, hs]
        for d in range(HG_BAND):
            rows = slice(pad - d, pad - d + C)
            for h, hs in enumerate(heads):
                decay = jnp.exp2(jnp.minimum(b[:, hs] - bpad_ref[slots[h], rows, :], 0.0))
                w = jnp.sum(q[:, hs] * kpad_ref[slots[h], rows, :] * decay, axis=-1, keepdims=True)
                outs[h] = outs[h] + jnp.where(sub >= d, w, 0.0) * vpad_ref[slots[h], rows, :]

        normed = [o * lax.rsqrt(jnp.mean(o * o, axis=-1, keepdims=True) + EPS) for o in outs]
        o_ref[rows_n, :] = (jnp.concatenate(normed, axis=1) * gn_ref[...]
                            * _silu(og_ref[rows_n, :])).astype(o_ref.dtype)

        b_last = b[C - 1:C, :]
        k_dec = (k * jnp.exp2(b_last - b)).astype(_BF16)
        keep = jnp.exp2(b_last)
        return [states[h] * keep[:, hs] + _dot_tn(v_bf[:, hs], k_dec[:, hs]) for h, hs in enumerate(heads)]

    states = [state_ref[h] for h in range(HG_HEADS)]
    for n in range(HG_STEP_CHUNKS):
        states = chunk(n, states)
    for h in range(HG_HEADS):
        state_ref[h] = states[h]


def hgrn2_mixer(rest3, hg_lb, hg_norm, layer):
    B, S, _ = rest3.shape
    depth = hg_lb.shape[0]
    C, W = HG_CHUNK, HG_HEAD_DIM
    rows = C * HG_STEP_CHUNKS

    def cols(base):
        return pl.BlockSpec((None, rows, HG_WIDTH), lambda b, c: (b, c, base * LANE // HG_WIDTH))

    windows = pltpu.VMEM((HG_STEP_CHUNKS * HG_HEADS, HG_PAD + C, W), _F32)
    return pl.pallas_call(
        functools.partial(_hg_kernel, layer),
        grid=(B, S // rows),
        in_specs=[cols(COL_ZF), cols(COL_HV), cols(COL_HQ), cols(COL_OG),
                  pl.BlockSpec((depth, 1, HG_WIDTH), lambda b, c: (0, 0, 0)),
                  pl.BlockSpec((None, 1, HG_WIDTH), lambda b, c: (layer, 0, 0))],
        out_specs=pl.BlockSpec((None, rows, HG_WIDTH), lambda b, c: (b, c, 0)),
        out_shape=jax.ShapeDtypeStruct((B, S, HG_WIDTH), _BF16),
        scratch_shapes=[pltpu.VMEM((HG_HEADS, W, W), _F32), windows, windows, windows],
        compiler_params=_params("arbitrary", "arbitrary"),
        name="hgrn2_mixer",
    )(rest3, rest3, rest3, rest3, hg_lb.reshape(depth, 1, HG_WIDTH), hg_norm.reshape(depth, 1, HG_WIDTH))


def _branch_mix_kernel(yp_ref, ys_ref, yh_ref, g0_ref, g1_ref, g2_ref, wp_ref, ws_ref, wh_ref, slab_ref,
                       o_ref, slab_bf_ref, wp_bf, ws_bf, wh_bf):
    @pl.when(pl.program_id(1) == 0)
    def _():
        wp_bf[...] = wp_ref[...].astype(_BF16)
        ws_bf[...] = ws_ref[...].astype(_BF16)
        wh_bf[...] = wh_ref[...].astype(_BF16)

    def gate(g_ref):
        return jax.nn.sigmoid(g_ref[...].astype(_F32))

    mixed = (gate(g0_ref) * _dot(yp_ref[...], wp_bf[...])
             + gate(g1_ref) * _dot(ys_ref[...], ws_bf[...])
             + gate(g2_ref) * _dot(yh_ref[...], wh_bf[...]))
    o_ref[...] = mixed.astype(o_ref.dtype)
    slab_bf_ref[...] = slab_ref[...].astype(_BF16)


def branch_mix(act, y_pool, y_sb, y_hg, w_pool, w_sb, w_hg, w_out, layer, tm=1024, tn=1024):
    T = act.shape[0]
    D = D_MODEL
    gate0 = QKV_WIDTH // tn
    per_gate = D // tn
    n_tok = T // tm
    slab_in, slab_out, slab_shape = _cast_slab_specs(w_out, layer, per_gate * n_tok, lambda n, i: n * n_tok + i)

    def gate(idx):
        return pl.BlockSpec((tm, tn), lambda n, i: (i, gate0 + idx * per_gate + n))

    def weight(width):
        return pl.BlockSpec((None, width, tn), lambda n, i: (layer, 0, n), pipeline_mode=pl.Buffered(1))

    return pl.pallas_call(
        _branch_mix_kernel,
        grid=(per_gate, n_tok),
        in_specs=[pl.BlockSpec((tm, POOL_WIDTH), lambda n, i: (i, 0)),
                  pl.BlockSpec((tm, SB_WIDTH), lambda n, i: (i, 0)),
                  pl.BlockSpec((tm, HG_WIDTH), lambda n, i: (i, 0)),
                  gate(0), gate(1), gate(2),
                  weight(POOL_WIDTH), weight(SB_WIDTH), weight(HG_WIDTH), slab_in],
        out_specs=[pl.BlockSpec((tm, tn), lambda n, i: (i, n)), slab_out],
        out_shape=[jax.ShapeDtypeStruct((T, D), _BF16), slab_shape],
        scratch_shapes=[pltpu.VMEM((POOL_WIDTH, tn), _BF16), pltpu.VMEM((SB_WIDTH, tn), _BF16),
                        pltpu.VMEM((HG_WIDTH, tn), _BF16)],
        compiler_params=_params("arbitrary", "arbitrary"),
        name="branch_mix",
    )(y_pool, y_sb, y_hg, act, act, act, w_pool, w_sb, w_hg, w_out)


def _out_proj_kernel(rows, x_ref, m_ref, w_ref, g_ref, o_ref, hn_ref):
    w = w_ref[...]
    for r0 in range(0, x_ref.shape[0], rows):
        rs = slice(r0, r0 + rows)
        out = x_ref[rs, :] + _dot(m_ref[rs, :], w)
        o_ref[rs, :] = out
        hn_ref[rs, :] = _rms_rows(out, g_ref[...]).astype(hn_ref.dtype)


def out_proj(x, mixed, w_out_bf, gains, layer, tm=512, rows=256):
    T, D = x.shape
    return pl.pallas_call(
        functools.partial(_out_proj_kernel, rows),
        grid=(T // tm,),
        in_specs=[pl.BlockSpec((tm, D), lambda i: (i, 0)),
                  pl.BlockSpec((tm, D), lambda i: (i, 0)),
                  pl.BlockSpec((D, D), lambda i: (0, 0), pipeline_mode=pl.Buffered(1)),
                  pl.BlockSpec((None, 1, D), _layer_row(layer))],
        out_specs=[pl.BlockSpec((tm, D), lambda i: (i, 0)), pl.BlockSpec((tm, D), lambda i: (i, 0))],
        out_shape=[jax.ShapeDtypeStruct((T, D), _F32), jax.ShapeDtypeStruct((T, D), _BF16)],
        compiler_params=_params("arbitrary"),
        name="out_proj",
    )(x, mixed, w_out_bf, gains.reshape(-1, 1, D))


def _ffn_up_kernel(h_ref, wg_ref, wu_ref, slab_ref, o_ref, slab_bf_ref):
    h = h_ref[...]
    gate = _dot(h, wg_ref[...].astype(_BF16))
    up = _dot(h, wu_ref[...].astype(_BF16))
    o_ref[...] = (_silu(gate) * up).astype(o_ref.dtype)
    slab_bf_ref[...] = slab_ref[...].astype(_BF16)


def ffn_up(h, w_gate_up, w_down, layer, tm=2048, tn=512):
    T, D = h.shape
    ni, nj = T // tm, D_FF // tn
    slab_in, slab_out, slab_shape = _cast_slab_specs(w_down, layer, ni * nj, lambda i, j: i * nj + j)
    return pl.pallas_call(
        _ffn_up_kernel,
        grid=(ni, nj),
        in_specs=[pl.BlockSpec((tm, D), lambda i, j: (i, 0)),
                  pl.BlockSpec((None, D, tn), lambda i, j: (layer, 0, j)),
                  pl.BlockSpec((None, D, tn), lambda i, j: (layer, 0, nj + j)),
                  slab_in],
        out_specs=[pl.BlockSpec((tm, tn), lambda i, j: (i, j)), slab_out],
        out_shape=[jax.ShapeDtypeStruct((T, D_FF), _BF16), slab_shape],
        compiler_params=_params("arbitrary", "arbitrary"),
        name="ffn_up",
    )(h, w_gate_up, w_gate_up, w_down)


def _ffn_down_kernel(x_ref, a_ref, w_ref, slab_ref, o_ref, slab_bf_ref):
    o_ref[...] = x_ref[...] + _dot(a_ref[...], w_ref[...])
    slab_bf_ref[...] = slab_ref[...].astype(_BF16)


def ffn_down(x, hidden, w_down_bf, w_ple_gate, layer, tm=1024, tn=512):
    T, D = x.shape
    ni, nj = T // tm, D // tn
    slab_in, slab_out, slab_shape = _cast_slab_specs(w_ple_gate, layer, ni * nj, lambda i, j: i * nj + j)
    return pl.pallas_call(
        _ffn_down_kernel,
        grid=(ni, nj),
        in_specs=[pl.BlockSpec((tm, tn), lambda i, j: (i, j)),
                  pl.BlockSpec((tm, D_FF), lambda i, j: (i, 0)),
                  pl.BlockSpec((D_FF, tn), lambda i, j: (0, j)),
                  slab_in],
        out_specs=[pl.BlockSpec((tm, tn), lambda i, j: (i, j)), slab_out],
        out_shape=[jax.ShapeDtypeStruct((T, D), _F32), slab_shape],
        compiler_params=_params("arbitrary", "arbitrary"),
        name="ffn_down",
    )(x, hidden, w_down_bf, w_ple_gate)


def _ple_kernel(tn, finish, x_ref, g_ref, p_ref, wg_ref, wp_ref, gf_ref, o_ref, hn_ref):
    j = pl.program_id(1)

    @pl.when(j == 0)
    def _():
        hn_ref[...] = _rms_rows(x_ref[...], g_ref[...]).astype(_BF16)

    gate = jax.nn.sigmoid(_dot(hn_ref[...], wg_ref[...]))
    emb = _dot(p_ref[...].astype(_BF16), wp_ref[...].astype(_BF16))
    cols = pl.ds(pl.multiple_of(j * tn, tn), tn)
    o_ref[:, cols] = x_ref[:, cols] + gate * emb

    if finish:
        @pl.when(j == pl.num_programs(1) - 1)
        def _():
            o_ref[...] = _rms_rows(o_ref[...], gf_ref[...])


def ple_gate(x, gains, p, w_gate_bf, w_proj, final_gain, layer, finish, tm=1024, tn=1024):
    T, D = x.shape
    return pl.pallas_call(
        functools.partial(_ple_kernel, tn, finish),
        grid=(T // tm, D // tn),
        in_specs=[pl.BlockSpec((tm, D), lambda i, j: (i, 0)),
                  pl.BlockSpec((None, 1, D), _layer_row(layer)),
                  pl.BlockSpec((None, tm, PLE_DIM), lambda i, j: (layer, i, 0)),
                  pl.BlockSpec((D, tn), lambda i, j: (0, j)),
                  pl.BlockSpec((None, PLE_DIM, tn), lambda i, j: (layer, 0, j)),
                  pl.BlockSpec((1, D), lambda i, j: (0, 0))],
        out_specs=pl.BlockSpec((tm, D), lambda i, j: (i, 0)),
        out_shape=jax.ShapeDtypeStruct((T, D), _F32),
        scratch_shapes=[pltpu.VMEM((tm, D), _BF16)],
        compiler_params=_params("arbitrary", "arbitrary"),
        name="ple_gate",
    )(x, gains.reshape(-1, 1, D), p.reshape(-1, T, PLE_DIM), w_gate_bf, w_proj, final_gain.reshape(1, D))


def kernel(x, p, norm_mix, w_in, pool_w, pool_scale, hg_lb, hg_norm, w_br_pool, w_br_sb, w_br_hg,
           w_out, norm_ffn, w_gate_up, w_down, norm_ple, w_ple_gate, w_ple_proj, norm_final):
    B, S, D = x.shape
    T = B * S
    depth = w_in.shape[0]
    xf = x.reshape(T, D)
    for i in range(depth):
        act, rest = norm_in_proj(xf, norm_mix, w_in, i)
        rest3 = rest.reshape(B, S, -1)
        y_pool = pool_mixer(rest3, pool_w[i].astype(_BF16), pool_scale[i])
        y_sb = stick_breaking(act.reshape(B, S, -1))
        y_hg = hgrn2_mixer(rest3, hg_lb, hg_norm, i)
        mixed, w_out_bf = branch_mix(act, y_pool.reshape(T, POOL_WIDTH), y_sb.reshape(T, SB_WIDTH),
                                     y_hg.reshape(T, HG_WIDTH), w_br_pool, w_br_sb, w_br_hg, w_out, i)
        xf, h_ffn = out_proj(xf, mixed, w_out_bf, norm_ffn, i)
        hidden, w_down_bf = ffn_up(h_ffn, w_gate_up, w_down, i)
        xf, w_ple_gate_bf = ffn_down(xf, hidden, w_down_bf, w_ple_gate, i)
        xf = ple_gate(xf, norm_ple, p, w_ple_gate_bf, w_ple_proj, norm_final, i, finish=(i == depth - 1))
    return xf.reshape(B, S, D)
```

```python
import functools

import jax
import jax.numpy as jnp
from jax import lax
from jax.experimental import pallas as pl
from jax.experimental.pallas import tpu as pltpu

D_MODEL = 2048
POOL_WIDTH = 512
POOL_WINDOWS = (2, 4, 8, 16)
POOL_GROUP_DIM = 128
SB_WIDTH = 1024
SB_HEAD_DIM = 128
SB_HEADS = 8
HG_WIDTH = 512
HG_HEAD_DIM = 128
HG_HEADS = 4
LB_FLOOR = 1e-20
D_FF = 5632
PLE_DIM = 256
EPS = 1e-6
IN_COLS = POOL_WIDTH + 3 * SB_WIDTH + 4 * HG_WIDTH + 3 * D_MODEL

LANE = 128

N_BRANCH = 3
QKV_WIDTH = 3 * SB_WIDTH
ACT_WIDTH = QKV_WIDTH + N_BRANCH * D_MODEL
REST_WIDTH = IN_COLS - ACT_WIDTH
COL_POOL = 0
COL_ZF = POOL_WIDTH // LANE
COL_HV = COL_ZF + HG_WIDTH // LANE
COL_HQ = COL_HV + HG_WIDTH // LANE
COL_OG = COL_HQ + HG_WIDTH // LANE

ATT_BLOCK = 128
ATT_GROUP = 3
ATT_HEADS = 8
ATT_QBLOCKS = 4
_LOG2E = 1.4426950408889634
ATT_LOG_ZERO = -105.0
HG_CHUNK = 128
HG_STEP_CHUNKS = 4
HG_BAND = 4
HG_PAD = 8
VMEM_LIMIT = 56 * 1024 * 1024
FFN_ROWS = 256
NORM_ROWS = 512
IN_PROJ_TILE = 512

_F32 = jnp.float32
_BF16 = jnp.bfloat16


def _params(*sem):
    return pltpu.CompilerParams(dimension_semantics=sem, vmem_limit_bytes=VMEM_LIMIT)


def _dot(a, b):
    return jnp.dot(a, b, preferred_element_type=_F32)


def _dot_nt(a, b):
    return lax.dot_general(a, b, (((1,), (1,)), ((), ())), preferred_element_type=_F32)


def _dot_tn(a, b):
    return lax.dot_general(a, b, (((0,), (0,)), ((), ())), preferred_element_type=_F32)


def _split_dot(lhs_fn, x):
    hi = x.astype(_BF16)
    lo = (x - hi.astype(_F32)).astype(_BF16)
    return lhs_fn(hi) + lhs_fn(lo)


def _rms_rows(x, g):
    return x * lax.rsqrt(jnp.mean(x * x, axis=-1, keepdims=True) + EPS) * g


def _silu(x):
    return x * jax.nn.sigmoid(x)


def _layer_row(layer):
    return lambda *_: (layer, 0, 0)


def _cast_slab_specs(w, layer, n_steps, step_of):
    rows = w.shape[1] // n_steps
    assert rows * n_steps == w.shape[1] and rows % 16 == 0
    in_spec = pl.BlockSpec((None, rows, w.shape[2]), lambda *idx: (layer, step_of(*idx), 0))
    out_spec = pl.BlockSpec((rows, w.shape[2]), lambda *idx: (step_of(*idx), 0))
    return in_spec, out_spec, jax.ShapeDtypeStruct(w.shape[1:], _BF16)


def _in_proj_kernel(x_ref, g_ref, w_ref, act_ref, rest_ref, hn_ref):
    def project(rows, h):
        r = _dot(h, w_ref[...].astype(_BF16))
        act_ref[rows, :] = r.astype(_BF16)
        rest_ref[rows, :] = r

    @pl.when(pl.program_id(1) == 0)
    def _():
        for r0 in range(0, x_ref.shape[0], NORM_ROWS):
            rs = slice(r0, r0 + NORM_ROWS)
            h = _rms_rows(x_ref[rs, :], g_ref[...]).astype(_BF16)
            hn_ref[rs, :] = h
            project(rs, h)

    @pl.when(pl.program_id(1) != 0)
    def _():
        project(slice(None), hn_ref[...])


def norm_in_proj(x, gains, w, layer, tm=2048, tn=IN_PROJ_TILE):
    T, D = x.shape
    n_pool, n_qkv, n_hg, n_gate = POOL_WIDTH // tn, QKV_WIDTH // tn, 4 * HG_WIDTH // tn, N_BRANCH * D // tn
    n_act, n_rest = n_qkv + n_gate, n_pool + n_hg

    def w_tile(i, j):
        tile = jnp.where(j < n_qkv, j + n_pool,
                         jnp.where(j < n_act, j + n_pool + n_hg,
                                   jnp.where(j < n_act + n_pool, j - n_act, j - n_gate)))
        return layer, 0, tile

    def act_tile(i, j):
        return i, jnp.minimum(j, n_act)

    def rest_tile(i, j):
        return i, jnp.where(j < n_act, n_rest, j - n_act)

    return pl.pallas_call(
        _in_proj_kernel,
        grid=(T // tm, n_act + n_rest),
        in_specs=[pl.BlockSpec((tm, D), lambda i, j: (i, 0), pipeline_mode=pl.Buffered(1)),
                  pl.BlockSpec((None, 1, D), _layer_row(layer)),
                  pl.BlockSpec((None, D, tn), w_tile)],
        out_specs=[pl.BlockSpec((tm, tn), act_tile), pl.BlockSpec((tm, tn), rest_tile)],
        out_shape=[jax.ShapeDtypeStruct((T, ACT_WIDTH + tn), _BF16),
                   jax.ShapeDtypeStruct((T, REST_WIDTH + tn), _F32)],
        scratch_shapes=[pltpu.VMEM((tm, D), _BF16)],
        compiler_params=_params("arbitrary", "arbitrary"),
        name="norm_in_proj",
    )(x, gains.reshape(-1, 1, D), w)


def _pool_kernel(u_ref, w_ref, s_ref, o_ref):
    S = u_ref.shape[0]
    t = lax.broadcasted_iota(jnp.int32, (S, POOL_GROUP_DIM), 0)
    for gi, win in enumerate(POOL_WINDOWS):
        cols = slice(gi * POOL_GROUP_DIM, (gi + 1) * POOL_GROUP_DIM)
        u = u_ref[:, cols]
        acc = u
        shift = 1
        while shift < win:
            acc = acc + jnp.where(t >= shift, pltpu.roll(acc, shift, axis=0), 0.0)
            shift *= 2
        cnt = jnp.minimum(t + 1, win).astype(_F32)
        mixed = (acc / cnt - u).astype(_BF16)
        y = _dot(mixed, w_ref[gi]) * s_ref[:, cols]
        o_ref[:, cols] = y.astype(o_ref.dtype)


def pool_mixer(proj3, pool_w, pool_scale):
    B, S, _ = proj3.shape
    return pl.pallas_call(
        _pool_kernel,
        grid=(B,),
        in_specs=[pl.BlockSpec((None, S, POOL_WIDTH), lambda b: (b, 0, COL_POOL)),
                  pl.BlockSpec((len(POOL_WINDOWS), POOL_GROUP_DIM, POOL_GROUP_DIM), lambda b: (0, 0, 0)),
                  pl.BlockSpec((1, POOL_WIDTH), lambda b: (0, 0))],
        out_specs=pl.BlockSpec((None, S, POOL_WIDTH), lambda b: (b, 0, 0)),
        out_shape=jax.ShapeDtypeStruct((B, S, POOL_WIDTH), _BF16),
        compiler_params=_params("arbitrary"),
        name="pool_mixer",
    )(proj3, pool_w, pool_scale.reshape(1, POOL_WIDTH))


def _sb_kernel(q_ref, k_ref, v_ref, o_ref):
    pair = pl.program_id(2)
    blk, group, heads, nq = ATT_BLOCK, ATT_GROUP, ATT_HEADS, ATT_QBLOCKS
    scale = SB_HEAD_DIM ** -0.5
    last_block = pair * nq + nq - 1
    q_all = q_ref[...]
    key_minus_query = (lax.broadcasted_iota(jnp.int32, (blk, blk), 1)
                       - lax.broadcasted_iota(jnp.int32, (blk, blk), 0))
    r2 = lax.broadcasted_iota(jnp.int32, (2 * blk, 2 * blk), 0) % blk
    c2 = lax.broadcasted_iota(jnp.int32, (2 * blk, 2 * blk), 1)
    neg_sums = jnp.where((r2 >= c2) | (c2 >= blk), -1.0, 0.0).astype(_BF16)

    def body(carry):
        step, _, accs, laters = carry
        diag_mask = key_minus_query < jnp.where(step == 0, 0, blk)
        parts = []
        for qb in range(nq):
            i = pair * nq + qb
            for hd in range(heads):
                cols = slice(hd * SB_HEAD_DIM, (hd + 1) * SB_HEAD_DIM)
                q = q_all[qb * blk:(qb + 1) * blk, cols]
                for u in range(group):
                    j = i - group * step - u
                    start = pl.multiple_of(jnp.maximum(j, 0) * blk, blk)
                    d = _dot_nt(q, k_ref[pl.ds(start, blk), cols])
                    z = d * scale
                    softplus = jnp.maximum(z, 0.0) + jnp.log(1.0 + jnp.exp2(jnp.abs(d) * (-scale * _LOG2E)))
                    if u == 0:
                        softplus = jnp.where(diag_mask, softplus, 0.0)
                    hi = softplus.astype(_BF16)
                    lo = (softplus - hi.astype(_F32)).astype(_BF16)
                    sums = _dot(jnp.concatenate([hi, lo], axis=1), neg_sums)
                    parts.append((qb * heads + hd, cols, u, j, start, z, sums))
        accs, laters = list(accs), list(laters)
        for slot, cols, u, j, start, z, sums in parts:
            a = jnp.exp(z + (sums[:, :blk] + laters[slot]))
            if u == 0:
                a = jnp.where(diag_mask, a, 0.0)
            vb = v_ref[pl.ds(start, blk), cols]
            vb = jnp.where(j >= 0, vb, jnp.zeros_like(vb))
            accs[slot] = accs[slot] + _dot(a.astype(_BF16), vb)
            laters[slot] = laters[slot] + sums[:, blk:]
        largest_later = jnp.max(functools.reduce(jnp.maximum, laters))
        return step + 1, largest_later, tuple(accs), tuple(laters)

    def unfinished(carry):
        step, largest_later, _, _ = carry
        return (step <= last_block // group) & (largest_later > ATT_LOG_ZERO)

    zeros = tuple(jnp.zeros((blk, blk), _F32) for _ in range(nq * heads))
    _, _, accs, _ = lax.while_loop(unfinished, body, (jnp.int32(0), jnp.float32(0.0), zeros, zeros))
    rows = [jnp.concatenate(accs[qb * heads:(qb + 1) * heads], axis=1) for qb in range(nq)]
    o_ref[...] = jnp.concatenate(rows, axis=0).astype(o_ref.dtype)


def stick_breaking(qkv3):
    B, S, _ = qkv3.shape
    heads = ATT_HEADS
    blk = ATT_BLOCK * ATT_QBLOCKS
    width = heads * SB_HEAD_DIM
    per = SB_WIDTH // width
    return pl.pallas_call(
        _sb_kernel,
        grid=(B, per, S // blk),
        in_specs=[pl.BlockSpec((None, blk, width), lambda b, h, i: (b, i, h)),
                  pl.BlockSpec((None, S, width), lambda b, h, i: (b, 0, per + h)),
                  pl.BlockSpec((None, S, width), lambda b, h, i: (b, 0, 2 * per + h))],
        out_specs=pl.BlockSpec((None, blk, width), lambda b, h, i: (b, i, h)),
        out_shape=jax.ShapeDtypeStruct((B, S, SB_WIDTH), _BF16),
        compiler_params=_params("arbitrary", "arbitrary", "arbitrary"),
        name="stick_breaking",
    )(qkv3, qkv3, qkv3)


def _hg_kernel(layer, zf_ref, hv_ref, hq_ref, og_ref, lb_ref, gn_ref, o_ref,
               state_ref, kpad_ref, bpad_ref, vpad_ref):
    C, W, pad = HG_CHUNK, HG_HEAD_DIM, HG_PAD
    heads = [slice(h * W, (h + 1) * W) for h in range(HG_HEADS)]

    @pl.when(pl.program_id(1) == 0)
    def _():
        state_ref[...] = jnp.zeros_like(state_ref)

    lbs = [lb_ref[d] for d in range(lb_ref.shape[0])]
    top = functools.reduce(jnp.maximum, lbs)
    es = [jnp.exp(row - top) for row in lbs]
    total = functools.reduce(jnp.add, es)
    sm = [e / total for e in es]
    lb = jnp.clip(functools.reduce(jnp.add, sm[:layer + 1]) - sm[0], 0.0, 1.0)

    r = lax.broadcasted_iota(jnp.int32, (C, C), 0)
    c = lax.broadcasted_iota(jnp.int32, (C, C), 1)
    tri = jnp.where(c <= r, 1.0, 0.0).astype(_BF16)
    sub = lax.broadcasted_iota(jnp.int32, (C, 1), 0) % HG_BAND
    zeros_pad = jnp.zeros((pad, W), _F32)

    def chunk(n, states):
        rows_n = slice(n * C, (n + 1) * C)
        z = zf_ref[rows_n, :]
        t = jnp.exp(-jnp.abs(z))
        big = 1.0 / (1.0 + t)
        small = t * big
        sig_pos = jnp.where(z >= 0.0, big, small)
        sig_neg = jnp.where(z >= 0.0, small, big)
        log_f = jnp.log(jnp.maximum(lb, LB_FLOOR) + (1.0 - jnp.minimum(lb, 1.0 - 1e-6)) * sig_pos)
        k = (1.0 - lb) * sig_neg
        q = _silu(hq_ref[rows_n, :])
        v = hv_ref[rows_n, :]
        v_bf = v.astype(_BF16)

        b = _split_dot(lambda part: _dot(tri, part), log_f) * _LOG2E

        q_dec = (q * jnp.exp2(b)).astype(_BF16)
        outs = [_dot_nt(q_dec[:, hs], states[h].astype(_BF16)) for h, hs in enumerate(heads)]

        scores = [jnp.zeros((C, C), _F32) for _ in heads]
        m = C // 2
        while m >= HG_BAND:
            span = 2 * m
            ref = jnp.concatenate(
                [jnp.broadcast_to(b[g * span + m - 1:g * span + m, :], (span, HG_WIDTH))
                 for g in range(C // span)], axis=0)
            qt = (q * jnp.exp2(jnp.minimum(b - ref, 0.0))).astype(_BF16)
            kt = (k * jnp.exp2(jnp.minimum(ref - b, 0.0))).astype(_BF16)
            sel = ((r // span) == (c // span)) & ((r % span) >= m) & ((c % span) < m)
            scores = [jnp.where(sel, _dot_nt(qt[:, hs], kt[:, hs]), scores[h]) for h, hs in enumerate(heads)]
            m //= 2
        outs = [outs[h] + _dot(scores[h].astype(_BF16), v_bf[:, hs]) for h, hs in enumerate(heads)]

        slots = [n * HG_HEADS + h for h in range(HG_HEADS)]
        for ref_, val in ((kpad_ref, k), (bpad_ref, b), (vpad_ref, v)):
            for h, hs in enumerate(heads):
                ref_[slots[h], 0:pad, :] = zeros_pad
                ref_[slots[h], pad:pad + C, :] = val[:, hs]
        for d in range(HG_BAND):
            rows = slice(pad - d, pad - d + C)
            for h, hs in enumerate(heads):
                decay = jnp.exp2(jnp.minimum(b[:, hs] - bpad_ref[slots[h], rows, :], 0.0))
                w = jnp.sum(q[:, hs] * kpad_ref[slots[h], rows, :] * decay, axis=-1, keepdims=True)
                outs[h] = outs[h] + jnp.where(sub >= d, w, 0.0) * vpad_ref[slots[h], rows, :]

        normed = [o * lax.rsqrt(jnp.mean(o * o, axis=-1, keepdims=True) + EPS) for o in outs]
        o_ref[rows_n, :] = (jnp.concatenate(normed, axis=1) * gn_ref[...]
                            * _silu(og_ref[rows_n, :])).astype(o_ref.dtype)

        b_last = b[C - 1:C, :]
        k_dec = (k * jnp.exp2(b_last - b)).astype(_BF16)
        keep = jnp.exp2(b_last)
        return [states[h] * keep[:, hs] + _dot_tn(v_bf[:, hs], k_dec[:, hs]) for h, hs in enumerate(heads)]

    states = [state_ref[h] for h in range(HG_HEADS)]
    for n in range(HG_STEP_CHUNKS):
        states = chunk(n, states)
    for h in range(HG_HEADS):
        state_ref[h] = states[h]


def hgrn2_mixer(rest3, hg_lb, hg_norm, layer):
    B, S, _ = rest3.shape
    depth = hg_lb.shape[0]
    C, W = HG_CHUNK, HG_HEAD_DIM
    rows = C * HG_STEP_CHUNKS

    def cols(base):
        return pl.BlockSpec((None, rows, HG_WIDTH), lambda b, c: (b, c, base * LANE // HG_WIDTH))

    windows = pltpu.VMEM((HG_STEP_CHUNKS * HG_HEADS, HG_PAD + C, W), _F32)
    return pl.pallas_call(
        functools.partial(_hg_kernel, layer),
        grid=(B, S // rows),
        in_specs=[cols(COL_ZF), cols(COL_HV), cols(COL_HQ), cols(COL_OG),
                  pl.BlockSpec((depth, 1, HG_WIDTH), lambda b, c: (0, 0, 0)),
                  pl.BlockSpec((None, 1, HG_WIDTH), lambda b, c: (layer, 0, 0))],
        out_specs=pl.BlockSpec((None, rows, HG_WIDTH), lambda b, c: (b, c, 0)),
        out_shape=jax.ShapeDtypeStruct((B, S, HG_WIDTH), _BF16),
        scratch_shapes=[pltpu.VMEM((HG_HEADS, W, W), _F32), windows, windows, windows],
        compiler_params=_params("arbitrary", "arbitrary"),
        name="hgrn2_mixer",
    )(rest3, rest3, rest3, rest3, hg_lb.reshape(depth, 1, HG_WIDTH), hg_norm.reshape(depth, 1, HG_WIDTH))


def _branch_mix_kernel(yp_ref, ys_ref, yh_ref, g0_ref, g1_ref, g2_ref, wp_ref, ws_ref, wh_ref, slab_ref,
                       o_ref, slab_bf_ref, wp_bf, ws_bf, wh_bf):
    @pl.when(pl.program_id(1) == 0)
    def _():
        wp_bf[...] = wp_ref[...].astype(_BF16)
        ws_bf[...] = ws_ref[...].astype(_BF16)
        wh_bf[...] = wh_ref[...].astype(_BF16)

    wp, ws, wh = wp_bf[...], ws_bf[...], wh_bf[...]
    for r0 in range(0, o_ref.shape[0], FFN_ROWS):
        rs = slice(r0, r0 + FFN_ROWS)

        def gated(g_ref, y_ref, w):
            return jax.nn.sigmoid(g_ref[rs, :].astype(_F32)) * _dot(y_ref[rs, :], w)

        mixed = gated(g0_ref, yp_ref, wp) + gated(g1_ref, ys_ref, ws) + gated(g2_ref, yh_ref, wh)
        o_ref[rs, :] = mixed.astype(o_ref.dtype)
    slab_bf_ref[...] = slab_ref[...].astype(_BF16)


def branch_mix(act, y_pool, y_sb, y_hg, w_pool, w_sb, w_hg, w_out, layer, tm=1024, tn=1024):
    T = act.shape[0]
    D = D_MODEL
    gate0 = QKV_WIDTH // tn
    per_gate = D // tn
    n_tok = T // tm
    slab_in, slab_out, slab_shape = _cast_slab_specs(w_out, layer, per_gate * n_tok, lambda n, i: n * n_tok + i)

    def gate(idx):
        return pl.BlockSpec((tm, tn), lambda n, i: (i, gate0 + idx * per_gate + n))

    def weight(width):
        return pl.BlockSpec((None, width, tn), lambda n, i: (layer, 0, n), pipeline_mode=pl.Buffered(1))

    return pl.pallas_call(
        _branch_mix_kernel,
        grid=(per_gate, n_tok),
        in_specs=[pl.BlockSpec((tm, POOL_WIDTH), lambda n, i: (i, 0)),
                  pl.BlockSpec((tm, SB_WIDTH), lambda n, i: (i, 0)),
                  pl.BlockSpec((tm, HG_WIDTH), lambda n, i: (i, 0)),
                  gate(0), gate(1), gate(2),
                  weight(POOL_WIDTH), weight(SB_WIDTH), weight(HG_WIDTH), slab_in],
        out_specs=[pl.BlockSpec((tm, tn), lambda n, i: (i, n)), slab_out],
        out_shape=[jax.ShapeDtypeStruct((T, D), _BF16), slab_shape],
        scratch_shapes=[pltpu.VMEM((POOL_WIDTH, tn), _BF16), pltpu.VMEM((SB_WIDTH, tn), _BF16),
                        pltpu.VMEM((HG_WIDTH, tn), _BF16)],
        compiler_params=_params("arbitrary", "arbitrary"),
        name="branch_mix",
    )(y_pool, y_sb, y_hg, act, act, act, w_pool, w_sb, w_hg, w_out)


def _out_proj_kernel(rows, x_ref, m_ref, w_ref, g_ref, o_ref, hn_ref):
    w = w_ref[...]
    for r0 in range(0, x_ref.shape[0], rows):
        rs = slice(r0, r0 + rows)
        out = x_ref[rs, :] + _dot(m_ref[rs, :], w)
        o_ref[rs, :] = out
        hn_ref[rs, :] = _rms_rows(out, g_ref[...]).astype(hn_ref.dtype)


def out_proj(x, mixed, w_out_bf, gains, layer, tm=512, rows=256):
    T, D = x.shape
    return pl.pallas_call(
        functools.partial(_out_proj_kernel, rows),
        grid=(T // tm,),
        in_specs=[pl.BlockSpec((tm, D), lambda i: (i, 0)),
                  pl.BlockSpec((tm, D), lambda i: (i, 0)),
                  pl.BlockSpec((D, D), lambda i: (0, 0), pipeline_mode=pl.Buffered(1)),
                  pl.BlockSpec((None, 1, D), _layer_row(layer))],
        out_specs=[pl.BlockSpec((tm, D), lambda i: (i, 0)), pl.BlockSpec((tm, D), lambda i: (i, 0))],
        out_shape=[jax.ShapeDtypeStruct((T, D), _F32), jax.ShapeDtypeStruct((T, D), _BF16)],
        compiler_params=_params("arbitrary"),
        name="out_proj",
    )(x, mixed, w_out_bf, gains.reshape(-1, 1, D))


def _ffn_up_kernel(h_ref, wg_ref, wu_ref, slab_ref, o_ref, slab_bf_ref):
    wg = wg_ref[...].astype(_BF16)
    wu = wu_ref[...].astype(_BF16)
    for r0 in range(0, h_ref.shape[0], FFN_ROWS):
        rs = slice(r0, r0 + FFN_ROWS)
        h = h_ref[rs, :]
        o_ref[rs, :] = (_silu(_dot(h, wg)) * _dot(h, wu)).astype(o_ref.dtype)
    slab_bf_ref[...] = slab_ref[...].astype(_BF16)


def ffn_up(h, w_gate_up, w_down, layer, tm=2048, tn=512):
    T, D = h.shape
    ni, nj = T // tm, D_FF // tn
    slab_in, slab_out, slab_shape = _cast_slab_specs(w_down, layer, ni * nj, lambda i, j: i * nj + j)
    return pl.pallas_call(
        _ffn_up_kernel,
        grid=(ni, nj),
        in_specs=[pl.BlockSpec((tm, D), lambda i, j: (i, 0)),
                  pl.BlockSpec((None, D, tn), lambda i, j: (layer, 0, j)),
                  pl.BlockSpec((None, D, tn), lambda i, j: (layer, 0, nj + j)),
                  slab_in],
        out_specs=[pl.BlockSpec((tm, tn), lambda i, j: (i, j)), slab_out],
        out_shape=[jax.ShapeDtypeStruct((T, D_FF), _BF16), slab_shape],
        compiler_params=_params("arbitrary", "arbitrary"),
        name="ffn_up",
    )(h, w_gate_up, w_gate_up, w_down)


def _ffn_down_kernel(x_ref, a_ref, w_ref, slab_ref, o_ref, slab_bf_ref):
    o_ref[...] = x_ref[...] + _dot(a_ref[...], w_ref[...])
    slab_bf_ref[...] = slab_ref[...].astype(_BF16)


def ffn_down(x, hidden, w_down_bf, w_ple_gate, layer, tm=1024, tn=512):
    T, D = x.shape
    ni, nj = T // tm, D // tn
    slab_in, slab_out, slab_shape = _cast_slab_specs(w_ple_gate, layer, ni * nj, lambda i, j: i * nj + j)
    return pl.pallas_call(
        _ffn_down_kernel,
        grid=(ni, nj),
        in_specs=[pl.BlockSpec((tm, tn), lambda i, j: (i, j)),
                  pl.BlockSpec((tm, D_FF), lambda i, j: (i, 0)),
                  pl.BlockSpec((D_FF, tn), lambda i, j: (0, j)),
                  slab_in],
        out_specs=[pl.BlockSpec((tm, tn), lambda i, j: (i, j)), slab_out],
        out_shape=[jax.ShapeDtypeStruct((T, D), _F32), slab_shape],
        compiler_params=_params("arbitrary", "arbitrary"),
        name="ffn_down",
    )(x, hidden, w_down_bf, w_ple_gate)


def _ple_kernel(tn, finish, x_ref, g_ref, p_ref, wg_ref, wp_ref, gf_ref, o_ref, hn_ref):
    j = pl.program_id(1)
    cols = pl.ds(pl.multiple_of(j * tn, tn), tn)

    def column_tile(normalise):
        wg = wg_ref[...]
        wp = wp_ref[...].astype(_BF16)
        for r0 in range(0, x_ref.shape[0], FFN_ROWS):
            rs = slice(r0, r0 + FFN_ROWS)
            if normalise:
                hn_ref[rs, :] = _rms_rows(x_ref[rs, :], g_ref[...]).astype(_BF16)
            gate = jax.nn.sigmoid(_dot(hn_ref[rs, :], wg))
            emb = _dot(p_ref[rs, :].astype(_BF16), wp)
            o_ref[rs, cols] = x_ref[rs, cols] + gate * emb

    @pl.when(j == 0)
    def _():
        column_tile(True)

    @pl.when(j != 0)
    def _():
        column_tile(False)

    if finish:
        @pl.when(j == pl.num_programs(1) - 1)
        def _():
            o_ref[...] = _rms_rows(o_ref[...], gf_ref[...])


def ple_gate(x, gains, p, w_gate_bf, w_proj, final_gain, layer, finish, tm=1024, tn=1024):
    T, D = x.shape
    return pl.pallas_call(
        functools.partial(_ple_kernel, tn, finish),
        grid=(T // tm, D // tn),
        in_specs=[pl.BlockSpec((tm, D), lambda i, j: (i, 0)),
                  pl.BlockSpec((None, 1, D), _layer_row(layer)),
                  pl.BlockSpec((None, tm, PLE_DIM), lambda i, j: (layer, i, 0)),
                  pl.BlockSpec((D, tn), lambda i, j: (0, j)),
                  pl.BlockSpec((None, PLE_DIM, tn), lambda i, j: (layer, 0, j)),
                  pl.BlockSpec((1, D), lambda i, j: (0, 0))],
        out_specs=pl.BlockSpec((tm, D), lambda i, j: (i, 0)),
        out_shape=jax.ShapeDtypeStruct((T, D), _F32),
        scratch_shapes=[pltpu.VMEM((tm, D), _BF16)],
        compiler_params=_params("arbitrary", "arbitrary"),
        name="ple_gate",
    )(x, gains.reshape(-1, 1, D), p.reshape(-1, T, PLE_DIM), w_gate_bf, w_proj, final_gain.reshape(1, D))


def kernel(x, p, norm_mix, w_in, pool_w, pool_scale, hg_lb, hg_norm, w_br_pool, w_br_sb, w_br_hg,
           w_out, norm_ffn, w_gate_up, w_down, norm_ple, w_ple_gate, w_ple_proj, norm_final):
    B, S, D = x.shape
    T = B * S
    depth = w_in.shape[0]
    xf = x.reshape(T, D)
    for i in range(depth):
        act, rest = norm_in_proj(xf, norm_mix, w_in, i)
        rest3 = rest.reshape(B, S, -1)
        y_pool = pool_mixer(rest3, pool_w[i].astype(_BF16), pool_scale[i])
        y_sb = stick_breaking(act.reshape(B, S, -1))
        y_hg = hgrn2_mixer(rest3, hg_lb, hg_norm, i)
        mixed, w_out_bf = branch_mix(act, y_pool.reshape(T, POOL_WIDTH), y_sb.reshape(T, SB_WIDTH),
                                     y_hg.reshape(T, HG_WIDTH), w_br_pool, w_br_sb, w_br_hg, w_out, i)
        xf, h_ffn = out_proj(xf, mixed, w_out_bf, norm_ffn, i)
        hidden, w_down_bf = ffn_up(h_ffn, w_gate_up, w_down, i)
        xf, w_ple_gate_bf = ffn_down(xf, hidden, w_down_bf, w_ple_gate, i)
        xf = ple_gate(xf, norm_ple, p, w_ple_gate_bf, w_ple_proj, norm_final, i, finish=(i == depth - 1))
    return xf.reshape(B, S, D)
```

```python
import functools

import jax
import jax.numpy as jnp
from jax import lax
from jax.experimental import pallas as pl
from jax.experimental.pallas import tpu as pltpu

D_MODEL = 2048
POOL_WIDTH = 512
POOL_WINDOWS = (2, 4, 8, 16)
POOL_GROUP_DIM = 128
SB_WIDTH = 1024
SB_HEAD_DIM = 128
SB_HEADS = 8
HG_WIDTH = 512
HG_HEAD_DIM = 128
HG_HEADS = 4
LB_FLOOR = 1e-20
D_FF = 5632
PLE_DIM = 256
EPS = 1e-6
IN_COLS = POOL_WIDTH + 3 * SB_WIDTH + 4 * HG_WIDTH + 3 * D_MODEL

LANE = 128

N_BRANCH = 3
QKV_WIDTH = 3 * SB_WIDTH
ACT_WIDTH = QKV_WIDTH + N_BRANCH * D_MODEL
REST_WIDTH = IN_COLS - ACT_WIDTH
COL_POOL = 0
COL_ZF = POOL_WIDTH // LANE
COL_HV = COL_ZF + HG_WIDTH // LANE
COL_HQ = COL_HV + HG_WIDTH // LANE
COL_OG = COL_HQ + HG_WIDTH // LANE

ATT_BLOCK = 128
ATT_GROUP = 3
ATT_HEADS = 8
ATT_QBLOCKS = 4
_LOG2E = 1.4426950408889634
ATT_LOG_ZERO = -105.0
HG_CHUNK = 128
HG_STEP_CHUNKS = 4
HG_BAND = 4
HG_PAD = 8
VMEM_LIMIT = 56 * 1024 * 1024
FFN_ROWS = 256
NORM_ROWS = 512
IN_PROJ_TILE = 512

_F32 = jnp.float32
_BF16 = jnp.bfloat16


def _params(*sem):
    return pltpu.CompilerParams(dimension_semantics=sem, vmem_limit_bytes=VMEM_LIMIT)


def _dot(a, b):
    return jnp.dot(a, b, preferred_element_type=_F32)


def _dot_nt(a, b):
    return lax.dot_general(a, b, (((1,), (1,)), ((), ())), preferred_element_type=_F32)


def _dot_tn(a, b):
    return lax.dot_general(a, b, (((0,), (0,)), ((), ())), preferred_element_type=_F32)


def _split_dot(lhs_fn, x):
    hi = x.astype(_BF16)
    lo = (x - hi.astype(_F32)).astype(_BF16)
    return lhs_fn(hi) + lhs_fn(lo)


def _rms_rows(x, g):
    return x * lax.rsqrt(jnp.mean(x * x, axis=-1, keepdims=True) + EPS) * g


def _silu(x):
    return x * jax.nn.sigmoid(x)


def _layer_row(layer):
    return lambda *_: (layer, 0, 0)


def _cast_slab_specs(w, layer, n_steps, step_of):
    rows = w.shape[1] // n_steps
    assert rows * n_steps == w.shape[1] and rows % 16 == 0
    in_spec = pl.BlockSpec((None, rows, w.shape[2]), lambda *idx: (layer, step_of(*idx), 0))
    out_spec = pl.BlockSpec((rows, w.shape[2]), lambda *idx: (step_of(*idx), 0))
    return in_spec, out_spec, jax.ShapeDtypeStruct(w.shape[1:], _BF16)


def _in_proj_kernel(x_hbm, g_ref, w_ref, act_ref, rest_ref, hn_ref, x_ref, x_sem):
    i, j = pl.program_id(0), pl.program_id(1)
    tm = x_ref.shape[0]

    def x_tile_copy(tile):
        return pltpu.make_async_copy(x_hbm.at[pl.ds(tile * tm, tm), :], x_ref, x_sem)

    def project(rows, h):
        r = _dot(h, w_ref[...].astype(_BF16))
        act_ref[rows, :] = r.astype(_BF16)
        rest_ref[rows, :] = r

    @pl.when(j == 0)
    def _():
        @pl.when(i == 0)
        def _():
            x_tile_copy(0).start()

        x_tile_copy(i).wait()
        for r0 in range(0, tm, NORM_ROWS):
            rs = slice(r0, r0 + NORM_ROWS)
            h = _rms_rows(x_ref[rs, :], g_ref[...]).astype(_BF16)
            hn_ref[rs, :] = h
            project(rs, h)

    @pl.when(j != 0)
    def _():
        @pl.when((j == 1) & (i + 1 < pl.num_programs(0)))
        def _():
            x_tile_copy(i + 1).start()

        project(slice(None), hn_ref[...])


def norm_in_proj(x, gains, w, layer, tm=2048, tn=IN_PROJ_TILE):
    T, D = x.shape
    n_pool, n_qkv, n_hg, n_gate = POOL_WIDTH // tn, QKV_WIDTH // tn, 4 * HG_WIDTH // tn, N_BRANCH * D // tn
    n_act, n_rest = n_qkv + n_gate, n_pool + n_hg

    def w_tile(i, j):
        tile = jnp.where(j < n_qkv, j + n_pool,
                         jnp.where(j < n_act, j + n_pool + n_hg,
                                   jnp.where(j < n_act + n_pool, j - n_act, j - n_gate)))
        return layer, 0, tile

    def act_tile(i, j):
        return i, jnp.minimum(j, n_act)

    def rest_tile(i, j):
        return i, jnp.where(j < n_act, n_rest, j - n_act)

    return pl.pallas_call(
        _in_proj_kernel,
        grid=(T // tm, n_act + n_rest),
        in_specs=[pl.BlockSpec(memory_space=pl.ANY),
                  pl.BlockSpec((None, 1, D), _layer_row(layer)),
                  pl.BlockSpec((None, D, tn), w_tile)],
        out_specs=[pl.BlockSpec((tm, tn), act_tile), pl.BlockSpec((tm, tn), rest_tile)],
        out_shape=[jax.ShapeDtypeStruct((T, ACT_WIDTH + tn), _BF16),
                   jax.ShapeDtypeStruct((T, REST_WIDTH + tn), _F32)],
        scratch_shapes=[pltpu.VMEM((tm, D), _BF16), pltpu.VMEM((tm, D), _F32), pltpu.SemaphoreType.DMA(())],
        compiler_params=_params("arbitrary", "arbitrary"),
        name="norm_in_proj",
    )(x, gains.reshape(-1, 1, D), w)


def _pool_kernel(u_ref, w_ref, s_ref, o_ref):
    S = u_ref.shape[0]
    t = lax.broadcasted_iota(jnp.int32, (S, POOL_GROUP_DIM), 0)
    for gi, win in enumerate(POOL_WINDOWS):
        cols = slice(gi * POOL_GROUP_DIM, (gi + 1) * POOL_GROUP_DIM)
        u = u_ref[:, cols]
        acc = u
        shift = 1
        while shift < win:
            acc = acc + jnp.where(t >= shift, pltpu.roll(acc, shift, axis=0), 0.0)
            shift *= 2
        cnt = jnp.minimum(t + 1, win).astype(_F32)
        mixed = (acc / cnt - u).astype(_BF16)
        y = _dot(mixed, w_ref[gi]) * s_ref[:, cols]
        o_ref[:, cols] = y.astype(o_ref.dtype)


def pool_mixer(proj3, pool_w, pool_scale):
    B, S, _ = proj3.shape
    return pl.pallas_call(
        _pool_kernel,
        grid=(B,),
        in_specs=[pl.BlockSpec((None, S, POOL_WIDTH), lambda b: (b, 0, COL_POOL)),
                  pl.BlockSpec((len(POOL_WINDOWS), POOL_GROUP_DIM, POOL_GROUP_DIM), lambda b: (0, 0, 0)),
                  pl.BlockSpec((1, POOL_WIDTH), lambda b: (0, 0))],
        out_specs=pl.BlockSpec((None, S, POOL_WIDTH), lambda b: (b, 0, 0)),
        out_shape=jax.ShapeDtypeStruct((B, S, POOL_WIDTH), _BF16),
        compiler_params=_params("arbitrary"),
        name="pool_mixer",
    )(proj3, pool_w, pool_scale.reshape(1, POOL_WIDTH))


def _sb_kernel(q_ref, k_ref, v_ref, o_ref):
    pair = pl.program_id(2)
    blk, group, heads, nq = ATT_BLOCK, ATT_GROUP, ATT_HEADS, ATT_QBLOCKS
    scale = SB_HEAD_DIM ** -0.5
    last_block = pair * nq + nq - 1
    q_all = q_ref[...]
    key_minus_query = (lax.broadcasted_iota(jnp.int32, (blk, blk), 1)
                       - lax.broadcasted_iota(jnp.int32, (blk, blk), 0))
    r2 = lax.broadcasted_iota(jnp.int32, (2 * blk, 2 * blk), 0) % blk
    c2 = lax.broadcasted_iota(jnp.int32, (2 * blk, 2 * blk), 1)
    neg_sums = jnp.where((r2 >= c2) | (c2 >= blk), -1.0, 0.0).astype(_BF16)

    def body(carry):
        step, _, accs, laters = carry
        diag_mask = key_minus_query < jnp.where(step == 0, 0, blk)
        parts = []
        for qb in range(nq):
            i = pair * nq + qb
            for hd in range(heads):
                cols = slice(hd * SB_HEAD_DIM, (hd + 1) * SB_HEAD_DIM)
                q = q_all[qb * blk:(qb + 1) * blk, cols]
                for u in range(group):
                    j = i - group * step - u
                    start = pl.multiple_of(jnp.maximum(j, 0) * blk, blk)
                    d = _dot_nt(q, k_ref[pl.ds(start, blk), cols])
                    z = d * scale
                    softplus = jnp.maximum(z, 0.0) + jnp.log(1.0 + jnp.exp2(jnp.abs(d) * (-scale * _LOG2E)))
                    if u == 0:
                        softplus = jnp.where(diag_mask, softplus, 0.0)
                    hi = softplus.astype(_BF16)
                    lo = (softplus - hi.astype(_F32)).astype(_BF16)
                    sums = _dot(jnp.concatenate([hi, lo], axis=1), neg_sums)
                    parts.append((qb * heads + hd, cols, u, j, start, z, sums))
        accs, laters = list(accs), list(laters)
        for slot, cols, u, j, start, z, sums in parts:
            a = jnp.exp(z + (sums[:, :blk] + laters[slot]))
            if u == 0:
                a = jnp.where(diag_mask, a, 0.0)
            vb = v_ref[pl.ds(start, blk), cols]
            vb = jnp.where(j >= 0, vb, jnp.zeros_like(vb))
            accs[slot] = accs[slot] + _dot(a.astype(_BF16), vb)
            laters[slot] = laters[slot] + sums[:, blk:]
        largest_later = jnp.max(functools.reduce(jnp.maximum, laters))
        return step + 1, largest_later, tuple(accs), tuple(laters)

    def unfinished(carry):
        step, largest_later, _, _ = carry
        return (step <= last_block // group) & (largest_later > ATT_LOG_ZERO)

    zeros = tuple(jnp.zeros((blk, blk), _F32) for _ in range(nq * heads))
    _, _, accs, _ = lax.while_loop(unfinished, body, (jnp.int32(0), jnp.float32(0.0), zeros, zeros))
    rows = [jnp.concatenate(accs[qb * heads:(qb + 1) * heads], axis=1) for qb in range(nq)]
    o_ref[...] = jnp.concatenate(rows, axis=0).astype(o_ref.dtype)


def stick_breaking(qkv3):
    B, S, _ = qkv3.shape
    heads = ATT_HEADS
    blk = ATT_BLOCK * ATT_QBLOCKS
    width = heads * SB_HEAD_DIM
    per = SB_WIDTH // width
    return pl.pallas_call(
        _sb_kernel,
        grid=(B, per, S // blk),
        in_specs=[pl.BlockSpec((None, blk, width), lambda b, h, i: (b, i, h)),
                  pl.BlockSpec((None, S, width), lambda b, h, i: (b, 0, per + h)),
                  pl.BlockSpec((None, S, width), lambda b, h, i: (b, 0, 2 * per + h))],
        out_specs=pl.BlockSpec((None, blk, width), lambda b, h, i: (b, i, h)),
        out_shape=jax.ShapeDtypeStruct((B, S, SB_WIDTH), _BF16),
        compiler_params=_params("arbitrary", "arbitrary", "arbitrary"),
        name="stick_breaking",
    )(qkv3, qkv3, qkv3)


def _hg_kernel(layer, zf_ref, hv_ref, hq_ref, og_ref, lb_ref, gn_ref, o_ref,
               state_ref, kpad_ref, bpad_ref, vpad_ref):
    C, W, pad = HG_CHUNK, HG_HEAD_DIM, HG_PAD
    heads = [slice(h * W, (h + 1) * W) for h in range(HG_HEADS)]

    @pl.when(pl.program_id(1) == 0)
    def _():
        state_ref[...] = jnp.zeros_like(state_ref)

    lbs = [lb_ref[d] for d in range(lb_ref.shape[0])]
    top = functools.reduce(jnp.maximum, lbs)
    es = [jnp.exp(row - top) for row in lbs]
    total = functools.reduce(jnp.add, es)
    sm = [e / total for e in es]
    lb = jnp.clip(functools.reduce(jnp.add, sm[:layer + 1]) - sm[0], 0.0, 1.0)

    r = lax.broadcasted_iota(jnp.int32, (C, C), 0)
    c = lax.broadcasted_iota(jnp.int32, (C, C), 1)
    tri = jnp.where(c <= r, 1.0, 0.0).astype(_BF16)
    sub = lax.broadcasted_iota(jnp.int32, (C, 1), 0) % HG_BAND
    zeros_pad = jnp.zeros((pad, W), _F32)

    def chunk(n, states):
        rows_n = slice(n * C, (n + 1) * C)
        z = zf_ref[rows_n, :]
        t = jnp.exp(-jnp.abs(z))
        big = 1.0 / (1.0 + t)
        small = t * big
        sig_pos = jnp.where(z >= 0.0, big, small)
        sig_neg = jnp.where(z >= 0.0, small, big)
        log_f = jnp.log(jnp.maximum(lb, LB_FLOOR) + (1.0 - jnp.minimum(lb, 1.0 - 1e-6)) * sig_pos)
        k = (1.0 - lb) * sig_neg
        q = _silu(hq_ref[rows_n, :])
        v = hv_ref[rows_n, :]
        v_bf = v.astype(_BF16)

        b = _split_dot(lambda part: _dot(tri, part), log_f) * _LOG2E

        q_dec = (q * jnp.exp2(b)).astype(_BF16)
        outs = [_dot_nt(q_dec[:, hs], states[h].astype(_BF16)) for h, hs in enumerate(heads)]

        scores = [jnp.zeros((C, C), _F32) for _ in heads]
        m = C // 2
        while m >= HG_BAND:
            span = 2 * m
            ref = jnp.concatenate(
                [jnp.broadcast_to(b[g * span + m - 1:g * span + m, :], (span, HG_WIDTH))
                 for g in range(C // span)], axis=0)
            qt = (q * jnp.exp2(jnp.minimum(b - ref, 0.0))).astype(_BF16)
            kt = (k * jnp.exp2(jnp.minimum(ref - b, 0.0))).astype(_BF16)
            sel = ((r // span) == (c // span)) & ((r % span) >= m) & ((c % span) < m)
            scores = [jnp.where(sel, _dot_nt(qt[:, hs], kt[:, hs]), scores[h]) for h, hs in enumerate(heads)]
            m //= 2
        outs = [outs[h] + _dot(scores[h].astype(_BF16), v_bf[:, hs]) for h, hs in enumerate(heads)]

        slots = [n * HG_HEADS + h for h in range(HG_HEADS)]
        for ref_, val in ((kpad_ref, k), (bpad_ref, b), (vpad_ref, v)):
            for h, hs in enumerate(heads):
                ref_[slots[h], 0:pad, :] = zeros_pad
                ref_[slots[h], pad:pad + C, :] = val[:, hs]
        for d in range(HG_BAND):
            rows = slice(pad - d, pad - d + C)
            for h, hs in enumerate(heads):
                decay = jnp.exp2(jnp.minimum(b[:, hs] - bpad_ref[slots[h], rows, :], 0.0))
                w = jnp.sum(q[:, hs] * kpad_ref[slots[h], rows, :] * decay, axis=-1, keepdims=True)
                outs[h] = outs[h] + jnp.where(sub >= d, w, 0.0) * vpad_ref[slots[h], rows, :]

        normed = [o * lax.rsqrt(jnp.mean(o * o, axis=-1, keepdims=True) + EPS) for o in outs]
        o_ref[rows_n, :] = (jnp.concatenate(normed, axis=1) * gn_ref[...]
                            * _silu(og_ref[rows_n, :])).astype(o_ref.dtype)

        b_last = b[C - 1:C, :]
        k_dec = (k * jnp.exp2(b_last - b)).astype(_BF16)
        keep = jnp.exp2(b_last)
        return [states[h] * keep[:, hs] + _dot_tn(v_bf[:, hs], k_dec[:, hs]) for h, hs in enumerate(heads)]

    states = [state_ref[h] for h in range(HG_HEADS)]
    for n in range(HG_STEP_CHUNKS):
        states = chunk(n, states)
    for h in range(HG_HEADS):
        state_ref[h] = states[h]


def hgrn2_mixer(rest3, hg_lb, hg_norm, layer):
    B, S, _ = rest3.shape
    depth = hg_lb.shape[0]
    C, W = HG_CHUNK, HG_HEAD_DIM
    rows = C * HG_STEP_CHUNKS

    def cols(base):
        return pl.BlockSpec((None, rows, HG_WIDTH), lambda b, c: (b, c, base * LANE // HG_WIDTH))

    windows = pltpu.VMEM((HG_STEP_CHUNKS * HG_HEADS, HG_PAD + C, W), _F32)
    return pl.pallas_call(
        functools.partial(_hg_kernel, layer),
        grid=(B, S // rows),
        in_specs=[cols(COL_ZF), cols(COL_HV), cols(COL_HQ), cols(COL_OG),
                  pl.BlockSpec((depth, 1, HG_WIDTH), lambda b, c: (0, 0, 0)),
                  pl.BlockSpec((None, 1, HG_WIDTH), lambda b, c: (layer, 0, 0))],
        out_specs=pl.BlockSpec((None, rows, HG_WIDTH), lambda b, c: (b, c, 0)),
        out_shape=jax.ShapeDtypeStruct((B, S, HG_WIDTH), _BF16),
        scratch_shapes=[pltpu.VMEM((HG_HEADS, W, W), _F32), windows, windows, windows],
        compiler_params=_params("arbitrary", "arbitrary"),
        name="hgrn2_mixer",
    )(rest3, rest3, rest3, rest3, hg_lb.reshape(depth, 1, HG_WIDTH), hg_norm.reshape(depth, 1, HG_WIDTH))


def _branch_mix_kernel(yp_ref, ys_ref, yh_ref, g0_ref, g1_ref, g2_ref, wp_ref, ws_ref, wh_ref, slab_ref,
                       o_ref, slab_bf_ref, wp_bf, ws_bf, wh_bf):
    @pl.when(pl.program_id(1) == 0)
    def _():
        wp_bf[...] = wp_ref[...].astype(_BF16)
        ws_bf[...] = ws_ref[...].astype(_BF16)
        wh_bf[...] = wh_ref[...].astype(_BF16)

    wp, ws, wh = wp_bf[...], ws_bf[...], wh_bf[...]
    for r0 in range(0, o_ref.shape[0], FFN_ROWS):
        rs = slice(r0, r0 + FFN_ROWS)

        def gated(g_ref, y_ref, w):
            return jax.nn.sigmoid(g_ref[rs, :].astype(_F32)) * _dot(y_ref[rs, :], w)

        mixed = gated(g0_ref, yp_ref, wp) + gated(g1_ref, ys_ref, ws) + gated(g2_ref, yh_ref, wh)
        o_ref[rs, :] = mixed.astype(o_ref.dtype)
    slab_bf_ref[...] = slab_ref[...].astype(_BF16)


def branch_mix(act, y_pool, y_sb, y_hg, w_pool, w_sb, w_hg, w_out, layer, tm=1024, tn=1024):
    T = act.shape[0]
    D = D_MODEL
    gate0 = QKV_WIDTH // tn
    per_gate = D // tn
    n_tok = T // tm
    slab_in, slab_out, slab_shape = _cast_slab_specs(w_out, layer, per_gate * n_tok, lambda n, i: n * n_tok + i)

    def gate(idx):
        return pl.BlockSpec((tm, tn), lambda n, i: (i, gate0 + idx * per_gate + n))

    def weight(width):
        return pl.BlockSpec((None, width, tn), lambda n, i: (layer, 0, n), pipeline_mode=pl.Buffered(1))

    return pl.pallas_call(
        _branch_mix_kernel,
        grid=(per_gate, n_tok),
        in_specs=[pl.BlockSpec((tm, POOL_WIDTH), lambda n, i: (i, 0)),
                  pl.BlockSpec((tm, SB_WIDTH), lambda n, i: (i, 0)),
                  pl.BlockSpec((tm, HG_WIDTH), lambda n, i: (i, 0)),
                  gate(0), gate(1), gate(2),
                  weight(POOL_WIDTH), weight(SB_WIDTH), weight(HG_WIDTH), slab_in],
        out_specs=[pl.BlockSpec((tm, tn), lambda n, i: (i, n)), slab_out],
        out_shape=[jax.ShapeDtypeStruct((T, D), _BF16), slab_shape],
        scratch_shapes=[pltpu.VMEM((POOL_WIDTH, tn), _BF16), pltpu.VMEM((SB_WIDTH, tn), _BF16),
                        pltpu.VMEM((HG_WIDTH, tn), _BF16)],
        compiler_params=_params("arbitrary", "arbitrary"),
        name="branch_mix",
    )(y_pool, y_sb, y_hg, act, act, act, w_pool, w_sb, w_hg, w_out)


def _out_proj_kernel(rows, x_ref, m_ref, w_ref, g_ref, o_ref, hn_ref):
    w = w_ref[...]
    for r0 in range(0, x_ref.shape[0], rows):
        rs = slice(r0, r0 + rows)
        out = x_ref[rs, :] + _dot(m_ref[rs, :], w)
        o_ref[rs, :] = out
        hn_ref[rs, :] = _rms_rows(out, g_ref[...]).astype(hn_ref.dtype)


def out_proj(x, mixed, w_out_bf, gains, layer, tm=512, rows=FFN_ROWS):
    T, D = x.shape
    return pl.pallas_call(
        functools.partial(_out_proj_kernel, rows),
        grid=(T // tm,),
        in_specs=[pl.BlockSpec((tm, D), lambda i: (i, 0)),
                  pl.BlockSpec((tm, D), lambda i: (i, 0)),
                  pl.BlockSpec((D, D), lambda i: (0, 0), pipeline_mode=pl.Buffered(1)),
                  pl.BlockSpec((None, 1, D), _layer_row(layer))],
        out_specs=[pl.BlockSpec((tm, D), lambda i: (i, 0)), pl.BlockSpec((tm, D), lambda i: (i, 0))],
        out_shape=[jax.ShapeDtypeStruct((T, D), _F32), jax.ShapeDtypeStruct((T, D), _BF16)],
        compiler_params=_params("arbitrary"),
        name="out_proj",
    )(x, mixed, w_out_bf, gains.reshape(-1, 1, D))


def _ffn_up_kernel(h_ref, wg_ref, wu_ref, slab_ref, o_ref, slab_bf_ref):
    wg = wg_ref[...].astype(_BF16)
    wu = wu_ref[...].astype(_BF16)
    for r0 in range(0, h_ref.shape[0], FFN_ROWS):
        rs = slice(r0, r0 + FFN_ROWS)
        h = h_ref[rs, :]
        o_ref[rs, :] = (_silu(_dot(h, wg)) * _dot(h, wu)).astype(o_ref.dtype)
    slab_bf_ref[...] = slab_ref[...].astype(_BF16)


def ffn_up(h, w_gate_up, w_down, layer, tm=2048, tn=512):
    T, D = h.shape
    ni, nj = T // tm, D_FF // tn
    slab_in, slab_out, slab_shape = _cast_slab_specs(w_down, layer, ni * nj, lambda i, j: i * nj + j)
    return pl.pallas_call(
        _ffn_up_kernel,
        grid=(ni, nj),
        in_specs=[pl.BlockSpec((tm, D), lambda i, j: (i, 0)),
                  pl.BlockSpec((None, D, tn), lambda i, j: (layer, 0, j)),
                  pl.BlockSpec((None, D, tn), lambda i, j: (layer, 0, nj + j)),
                  slab_in],
        out_specs=[pl.BlockSpec((tm, tn), lambda i, j: (i, j)), slab_out],
        out_shape=[jax.ShapeDtypeStruct((T, D_FF), _BF16), slab_shape],
        compiler_params=_params("arbitrary", "arbitrary"),
        name="ffn_up",
    )(h, w_gate_up, w_gate_up, w_down)


def _ffn_down_kernel(x_ref, a_ref, w_ref, slab_ref, o_ref, slab_bf_ref):
    o_ref[...] = x_ref[...] + _dot(a_ref[...], w_ref[...])
    slab_bf_ref[...] = slab_ref[...].astype(_BF16)


def ffn_down(x, hidden, w_down_bf, w_ple_gate, layer, tm=1024, tn=512):
    T, D = x.shape
    ni, nj = T // tm, D // tn
    slab_in, slab_out, slab_shape = _cast_slab_specs(w_ple_gate, layer, ni * nj, lambda i, j: i * nj + j)
    return pl.pallas_call(
        _ffn_down_kernel,
        grid=(ni, nj),
        in_specs=[pl.BlockSpec((tm, tn), lambda i, j: (i, j)),
                  pl.BlockSpec((tm, D_FF), lambda i, j: (i, 0)),
                  pl.BlockSpec((D_FF, tn), lambda i, j: (0, j)),
                  slab_in],
        out_specs=[pl.BlockSpec((tm, tn), lambda i, j: (i, j)), slab_out],
        out_shape=[jax.ShapeDtypeStruct((T, D), _F32), slab_shape],
        compiler_params=_params("arbitrary", "arbitrary"),
        name="ffn_down",
    )(x, hidden, w_down_bf, w_ple_gate)


def _ple_kernel(tn, finish, x_ref, g_ref, p_ref, wg_ref, wp_ref, gf_ref, o_ref, hn_ref):
    j = pl.program_id(1)
    cols = pl.ds(pl.multiple_of(j * tn, tn), tn)

    def column_tile(normalise):
        wg = wg_ref[...]
        wp = wp_ref[...].astype(_BF16)
        for r0 in range(0, x_ref.shape[0], FFN_ROWS):
            rs = slice(r0, r0 + FFN_ROWS)
            if normalise:
                hn_ref[rs, :] = _rms_rows(x_ref[rs, :], g_ref[...]).astype(_BF16)
            gate = jax.nn.sigmoid(_dot(hn_ref[rs, :], wg))
            emb = _dot(p_ref[rs, :].astype(_BF16), wp)
            o_ref[rs, cols] = x_ref[rs, cols] + gate * emb

    @pl.when(j == 0)
    def _():
        column_tile(True)

    @pl.when(j != 0)
    def _():
        column_tile(False)

    if finish:
        @pl.when(j == pl.num_programs(1) - 1)
        def _():
            o_ref[...] = _rms_rows(o_ref[...], gf_ref[...])


def ple_gate(x, gains, p, w_gate_bf, w_proj, final_gain, layer, finish, tm=1024, tn=1024):
    T, D = x.shape
    return pl.pallas_call(
        functools.partial(_ple_kernel, tn, finish),
        grid=(T // tm, D // tn),
        in_specs=[pl.BlockSpec((tm, D), lambda i, j: (i, 0)),
                  pl.BlockSpec((None, 1, D), _layer_row(layer)),
                  pl.BlockSpec((None, tm, PLE_DIM), lambda i, j: (layer, i, 0)),
                  pl.BlockSpec((D, tn), lambda i, j: (0, j)),
                  pl.BlockSpec((None, PLE_DIM, tn), lambda i, j: (layer, 0, j)),
                  pl.BlockSpec((1, D), lambda i, j: (0, 0))],
        out_specs=pl.BlockSpec((tm, D), lambda i, j: (i, 0)),
        out_shape=jax.ShapeDtypeStruct((T, D), _F32),
        scratch_shapes=[pltpu.VMEM((tm, D), _BF16)],
        compiler_params=_params("arbitrary", "arbitrary"),
        name="ple_gate",
    )(x, gains.reshape(-1, 1, D), p.reshape(-1, T, PLE_DIM), w_gate_bf, w_proj, final_gain.reshape(1, D))


def kernel(x, p, norm_mix, w_in, pool_w, pool_scale, hg_lb, hg_norm, w_br_pool, w_br_sb, w_br_hg,
           w_out, norm_ffn, w_gate_up, w_down, norm_ple, w_ple_gate, w_ple_proj, norm_final):
    B, S, D = x.shape
    T = B * S
    depth = w_in.shape[0]
    xf = x.reshape(T, D)
    for i in range(depth):
        act, rest = norm_in_proj(xf, norm_mix, w_in, i)
        rest3 = rest.reshape(B, S, -1)
        y_pool = pool_mixer(rest3, pool_w[i].astype(_BF16), pool_scale[i])
        y_sb = stick_breaking(act.reshape(B, S, -1))
        y_hg = hgrn2_mixer(rest3, hg_lb, hg_norm, i)
        mixed, w_out_bf = branch_mix(act, y_pool.reshape(T, POOL_WIDTH), y_sb.reshape(T, SB_WIDTH),
                                     y_hg.reshape(T, HG_WIDTH), w_br_pool, w_br_sb, w_br_hg, w_out, i)
        xf, h_ffn = out_proj(xf, mixed, w_out_bf, norm_ffn, i)
        hidden, w_down_bf = ffn_up(h_ffn, w_gate_up, w_down, i)
        xf, w_ple_gate_bf = ffn_down(xf, hidden, w_down_bf, w_ple_gate, i)
        xf = ple_gate(xf, norm_ple, p, w_ple_gate_bf, w_ple_proj, norm_final, i, finish=(i == depth - 1))
    return xf.reshape(B, S, D)
```

```python
import functools

import jax
import jax.numpy as jnp
from jax import lax
from jax.experimental import pallas as pl
from jax.experimental.pallas import tpu as pltpu

D_MODEL = 2048
POOL_WIDTH = 512
POOL_WINDOWS = (2, 4, 8, 16)
POOL_GROUP_DIM = 128
SB_WIDTH = 1024
SB_HEAD_DIM = 128
SB_HEADS = 8
HG_WIDTH = 512
HG_HEAD_DIM = 128
HG_HEADS = 4
LB_FLOOR = 1e-20
D_FF = 5632
PLE_DIM = 256
EPS = 1e-6
IN_COLS = POOL_WIDTH + 3 * SB_WIDTH + 4 * HG_WIDTH + 3 * D_MODEL

LANE = 128

N_BRANCH = 3
QKV_WIDTH = 3 * SB_WIDTH
ACT_WIDTH = QKV_WIDTH + N_BRANCH * D_MODEL
REST_WIDTH = IN_COLS - ACT_WIDTH
COL_POOL = 0
COL_ZF = POOL_WIDTH // LANE
COL_HV = COL_ZF + HG_WIDTH // LANE
COL_HQ = COL_HV + HG_WIDTH // LANE
COL_OG = COL_HQ + HG_WIDTH // LANE

ATT_BLOCK = 128
ATT_GROUP = 3
ATT_HEADS = 8
ATT_QBLOCKS = 4
_LOG2E = 1.4426950408889634
ATT_LOG_ZERO = -105.0
HG_CHUNK = 128
HG_STEP_CHUNKS = 4
HG_BAND = 4
HG_PAD = 8
VMEM_LIMIT = 56 * 1024 * 1024
FFN_ROWS = 256
NORM_ROWS = 512
IN_PROJ_TILE = 512

_F32 = jnp.float32
_BF16 = jnp.bfloat16


def _params(*sem):
    return pltpu.CompilerParams(dimension_semantics=sem, vmem_limit_bytes=VMEM_LIMIT)


def _dot(a, b):
    return jnp.dot(a, b, preferred_element_type=_F32)


def _dot_nt(a, b):
    return lax.dot_general(a, b, (((1,), (1,)), ((), ())), preferred_element_type=_F32)


def _dot_tn(a, b):
    return lax.dot_general(a, b, (((0,), (0,)), ((), ())), preferred_element_type=_F32)


def _split_dot(lhs_fn, x):
    hi = x.astype(_BF16)
    lo = (x - hi.astype(_F32)).astype(_BF16)
    return lhs_fn(hi) + lhs_fn(lo)


def _rms_rows(x, g):
    return x * lax.rsqrt(jnp.mean(x * x, axis=-1, keepdims=True) + EPS) * g


def _silu(x):
    return x * jax.nn.sigmoid(x)


def _layer_row(layer):
    return lambda *_: (layer, 0, 0)


def _cast_slab_specs(w, layer, n_steps, step_of):
    rows = w.shape[1] // n_steps
    assert rows * n_steps == w.shape[1] and rows % 16 == 0
    in_spec = pl.BlockSpec((None, rows, w.shape[2]), lambda *idx: (layer, step_of(*idx), 0))
    out_spec = pl.BlockSpec((rows, w.shape[2]), lambda *idx: (step_of(*idx), 0))
    return in_spec, out_spec, jax.ShapeDtypeStruct(w.shape[1:], _BF16)


def _in_proj_kernel(x_hbm, g_ref, w_ref, act_ref, rest_ref, hn_ref, x_ref, x_sem):
    i, j = pl.program_id(0), pl.program_id(1)
    tm = x_ref.shape[0]

    def x_tile_copy(tile):
        return pltpu.make_async_copy(x_hbm.at[pl.ds(tile * tm, tm), :], x_ref, x_sem)

    def project(rows, h):
        r = _dot(h, w_ref[...].astype(_BF16))
        act_ref[rows, :] = r.astype(_BF16)
        rest_ref[rows, :] = r

    @pl.when(j == 0)
    def _():
        @pl.when(i == 0)
        def _():
            x_tile_copy(0).start()

        x_tile_copy(i).wait()
        for r0 in range(0, tm, NORM_ROWS):
            rs = slice(r0, r0 + NORM_ROWS)
            h = _rms_rows(x_ref[rs, :], g_ref[...]).astype(_BF16)
            hn_ref[rs, :] = h
            project(rs, h)

    @pl.when(j != 0)
    def _():
        @pl.when((j == 1) & (i + 1 < pl.num_programs(0)))
        def _():
            x_tile_copy(i + 1).start()

        project(slice(None), hn_ref[...])


def norm_in_proj(x, gains, w, layer, tm=2048, tn=IN_PROJ_TILE):
    T, D = x.shape
    n_pool, n_qkv, n_hg, n_gate = POOL_WIDTH // tn, QKV_WIDTH // tn, 4 * HG_WIDTH // tn, N_BRANCH * D // tn
    n_act, n_rest = n_qkv + n_gate, n_pool + n_hg

    def w_tile(i, j):
        tile = jnp.where(j < n_qkv, j + n_pool,
                         jnp.where(j < n_act, j + n_pool + n_hg,
                                   jnp.where(j < n_act + n_pool, j - n_act, j - n_gate)))
        return layer, 0, tile

    def act_tile(i, j):
        return i, jnp.minimum(j, n_act)

    def rest_tile(i, j):
        return i, jnp.where(j < n_act, n_rest, j - n_act)

    return pl.pallas_call(
        _in_proj_kernel,
        grid=(T // tm, n_act + n_rest),
        in_specs=[pl.BlockSpec(memory_space=pl.ANY),
                  pl.BlockSpec((None, 1, D), _layer_row(layer)),
                  pl.BlockSpec((None, D, tn), w_tile)],
        out_specs=[pl.BlockSpec((tm, tn), act_tile), pl.BlockSpec((tm, tn), rest_tile)],
        out_shape=[jax.ShapeDtypeStruct((T, ACT_WIDTH + tn), _BF16),
                   jax.ShapeDtypeStruct((T, REST_WIDTH + tn), _F32)],
        scratch_shapes=[pltpu.VMEM((tm, D), _BF16), pltpu.VMEM((tm, D), _F32), pltpu.SemaphoreType.DMA(())],
        compiler_params=_params("arbitrary", "arbitrary"),
        name="norm_in_proj",
    )(x, gains.reshape(-1, 1, D), w)


def _pool_kernel(u_ref, w_ref, s_ref, o_ref):
    S = u_ref.shape[0]
    t = lax.broadcasted_iota(jnp.int32, (S, POOL_GROUP_DIM), 0)
    for gi, win in enumerate(POOL_WINDOWS):
        cols = slice(gi * POOL_GROUP_DIM, (gi + 1) * POOL_GROUP_DIM)
        u = u_ref[:, cols]
        acc = u
        shift = 1
        while shift < win:
            acc = acc + jnp.where(t >= shift, pltpu.roll(acc, shift, axis=0), 0.0)
            shift *= 2
        cnt = jnp.minimum(t + 1, win).astype(_F32)
        mixed = (acc / cnt - u).astype(_BF16)
        y = _dot(mixed, w_ref[gi]) * s_ref[:, cols]
        o_ref[:, cols] = y.astype(o_ref.dtype)


def pool_mixer(proj3, pool_w, pool_scale):
    B, S, _ = proj3.shape
    return pl.pallas_call(
        _pool_kernel,
        grid=(B,),
        in_specs=[pl.BlockSpec((None, S, POOL_WIDTH), lambda b: (b, 0, COL_POOL)),
                  pl.BlockSpec((len(POOL_WINDOWS), POOL_GROUP_DIM, POOL_GROUP_DIM), lambda b: (0, 0, 0)),
                  pl.BlockSpec((1, POOL_WIDTH), lambda b: (0, 0))],
        out_specs=pl.BlockSpec((None, S, POOL_WIDTH), lambda b: (b, 0, 0)),
        out_shape=jax.ShapeDtypeStruct((B, S, POOL_WIDTH), _BF16),
        compiler_params=_params("arbitrary"),
        name="pool_mixer",
    )(proj3, pool_w, pool_scale.reshape(1, POOL_WIDTH))


def _sb_kernel(q_ref, k_ref, v_ref, o_ref):
    pair = pl.program_id(2)
    blk, group, heads, nq = ATT_BLOCK, ATT_GROUP, ATT_HEADS, ATT_QBLOCKS
    scale = SB_HEAD_DIM ** -0.5
    last_block = pair * nq + nq - 1
    q_all = q_ref[...]
    key_minus_query = (lax.broadcasted_iota(jnp.int32, (blk, blk), 1)
                       - lax.broadcasted_iota(jnp.int32, (blk, blk), 0))
    r2 = lax.broadcasted_iota(jnp.int32, (2 * blk, 2 * blk), 0) % blk
    c2 = lax.broadcasted_iota(jnp.int32, (2 * blk, 2 * blk), 1)
    neg_sums = jnp.where((r2 >= c2) | (c2 >= blk), -1.0, 0.0).astype(_BF16)

    def body(carry):
        step, _, accs, laters = carry
        diag_mask = key_minus_query < jnp.where(step == 0, 0, blk)
        parts = []
        for qb in range(nq):
            i = pair * nq + qb
            for hd in range(heads):
                cols = slice(hd * SB_HEAD_DIM, (hd + 1) * SB_HEAD_DIM)
                q = q_all[qb * blk:(qb + 1) * blk, cols]
                for u in range(group):
                    j = i - group * step - u
                    start = pl.multiple_of(jnp.maximum(j, 0) * blk, blk)
                    d = _dot_nt(q, k_ref[pl.ds(start, blk), cols])
                    z = d * scale
                    softplus = jnp.maximum(z, 0.0) + jnp.log(1.0 + jnp.exp2(jnp.abs(d) * (-scale * _LOG2E)))
                    if u == 0:
                        softplus = jnp.where(diag_mask, softplus, 0.0)
                    hi = softplus.astype(_BF16)
                    lo = (softplus - hi.astype(_F32)).astype(_BF16)
                    sums = _dot(jnp.concatenate([hi, lo], axis=1), neg_sums)
                    parts.append((qb * heads + hd, cols, u, j, start, z, sums))
        accs, laters = list(accs), list(laters)
        for slot, cols, u, j, start, z, sums in parts:
            a = jnp.exp(z + (sums[:, :blk] + laters[slot]))
            if u == 0:
                a = jnp.where(diag_mask, a, 0.0)
            vb = v_ref[pl.ds(start, blk), cols]
            vb = jnp.where(j >= 0, vb, jnp.zeros_like(vb))
            accs[slot] = accs[slot] + _dot(a.astype(_BF16), vb)
            laters[slot] = laters[slot] + sums[:, blk:]
        largest_later = jnp.max(functools.reduce(jnp.maximum, laters))
        return step + 1, largest_later, tuple(accs), tuple(laters)

    def unfinished(carry):
        step, largest_later, _, _ = carry
        return (step <= last_block // group) & (largest_later > ATT_LOG_ZERO)

    zeros = tuple(jnp.zeros((blk, blk), _F32) for _ in range(nq * heads))
    _, _, accs, _ = lax.while_loop(unfinished, body, (jnp.int32(0), jnp.float32(0.0), zeros, zeros))
    rows = [jnp.concatenate(accs[qb * heads:(qb + 1) * heads], axis=1) for qb in range(nq)]
    o_ref[...] = jnp.concatenate(rows, axis=0).astype(o_ref.dtype)


def stick_breaking(qkv3):
    B, S, _ = qkv3.shape
    heads = ATT_HEADS
    blk = ATT_BLOCK * ATT_QBLOCKS
    width = heads * SB_HEAD_DIM
    per = SB_WIDTH // width
    return pl.pallas_call(
        _sb_kernel,
        grid=(B, per, S // blk),
        in_specs=[pl.BlockSpec((None, blk, width), lambda b, h, i: (b, i, h)),
                  pl.BlockSpec((None, S, width), lambda b, h, i: (b, 0, per + h)),
                  pl.BlockSpec((None, S, width), lambda b, h, i: (b, 0, 2 * per + h))],
        out_specs=pl.BlockSpec((None, blk, width), lambda b, h, i: (b, i, h)),
        out_shape=jax.ShapeDtypeStruct((B, S, SB_WIDTH), _BF16),
        compiler_params=_params("arbitrary", "arbitrary", "arbitrary"),
        name="stick_breaking",
    )(qkv3, qkv3, qkv3)


def _hg_kernel(layer, zf_ref, hv_ref, hq_ref, og_ref, lb_ref, gn_ref, o_ref,
               state_ref, kpad_ref, bpad_ref, vpad_ref):
    C, W, pad = HG_CHUNK, HG_HEAD_DIM, HG_PAD
    heads = [slice(h * W, (h + 1) * W) for h in range(HG_HEADS)]

    @pl.when(pl.program_id(1) == 0)
    def _():
        state_ref[...] = jnp.zeros_like(state_ref)

    lbs = [lb_ref[d] for d in range(lb_ref.shape[0])]
    top = functools.reduce(jnp.maximum, lbs)
    es = [jnp.exp(row - top) for row in lbs]
    total = functools.reduce(jnp.add, es)
    sm = [e / total for e in es]
    lb = jnp.clip(functools.reduce(jnp.add, sm[:layer + 1]) - sm[0], 0.0, 1.0)

    r = lax.broadcasted_iota(jnp.int32, (C, C), 0)
    c = lax.broadcasted_iota(jnp.int32, (C, C), 1)
    tri = jnp.where(c <= r, 1.0, 0.0).astype(_BF16)
    sub = lax.broadcasted_iota(jnp.int32, (C, 1), 0) % HG_BAND
    zeros_pad = jnp.zeros((pad, W), _F32)

    def chunk(n, states):
        rows_n = slice(n * C, (n + 1) * C)
        z = zf_ref[rows_n, :]
        t = jnp.exp(-jnp.abs(z))
        big = 1.0 / (1.0 + t)
        small = t * big
        sig_pos = jnp.where(z >= 0.0, big, small)
        sig_neg = jnp.where(z >= 0.0, small, big)
        log_f = jnp.log(jnp.maximum(lb, LB_FLOOR) + (1.0 - jnp.minimum(lb, 1.0 - 1e-6)) * sig_pos)
        k = (1.0 - lb) * sig_neg
        q = _silu(hq_ref[rows_n, :])
        v = hv_ref[rows_n, :]
        v_bf = v.astype(_BF16)

        b = _split_dot(lambda part: _dot(tri, part), log_f) * _LOG2E

        q_dec = (q * jnp.exp2(b)).astype(_BF16)
        outs = [_dot_nt(q_dec[:, hs], states[h].astype(_BF16)) for h, hs in enumerate(heads)]

        scores = [jnp.zeros((C, C), _F32) for _ in heads]
        m = C // 2
        while m >= HG_BAND:
            span = 2 * m
            ref = jnp.concatenate(
                [jnp.broadcast_to(b[g * span + m - 1:g * span + m, :], (span, HG_WIDTH))
                 for g in range(C // span)], axis=0)
            qt = (q * jnp.exp2(jnp.minimum(b - ref, 0.0))).astype(_BF16)
            kt = (k * jnp.exp2(jnp.minimum(ref - b, 0.0))).astype(_BF16)
            sel = ((r // span) == (c // span)) & ((r % span) >= m) & ((c % span) < m)
            scores = [jnp.where(sel, _dot_nt(qt[:, hs], kt[:, hs]), scores[h]) for h, hs in enumerate(heads)]
            m //= 2
        outs = [outs[h] + _dot(scores[h].astype(_BF16), v_bf[:, hs]) for h, hs in enumerate(heads)]

        slots = [n * HG_HEADS + h for h in range(HG_HEADS)]
        for ref_, val in ((kpad_ref, k), (bpad_ref, b), (vpad_ref, v)):
            for h, hs in enumerate(heads):
                ref_[slots[h], 0:pad, :] = zeros_pad
                ref_[slots[h], pad:pad + C, :] = val[:, hs]
        for d in range(HG_BAND):
            rows = slice(pad - d, pad - d + C)
            for h, hs in enumerate(heads):
                decay = jnp.exp2(jnp.minimum(b[:, hs] - bpad_ref[slots[h], rows, :], 0.0))
                w = jnp.sum(q[:, hs] * kpad_ref[slots[h], rows, :] * decay, axis=-1, keepdims=True)
                outs[h] = outs[h] + jnp.where(sub >= d, w, 0.0) * vpad_ref[slots[h], rows, :]

        normed = [o * lax.rsqrt(jnp.mean(o * o, axis=-1, keepdims=True) + EPS) for o in outs]
        o_ref[rows_n, :] = (jnp.concatenate(normed, axis=1) * gn_ref[...]
                            * _silu(og_ref[rows_n, :])).astype(o_ref.dtype)

        b_last = b[C - 1:C, :]
        k_dec = (k * jnp.exp2(b_last - b)).astype(_BF16)
        keep = jnp.exp2(b_last)
        return [states[h] * keep[:, hs] + _dot_tn(v_bf[:, hs], k_dec[:, hs]) for h, hs in enumerate(heads)]

    states = [state_ref[h] for h in range(HG_HEADS)]
    for n in range(HG_STEP_CHUNKS):
        states = chunk(n, states)
    for h in range(HG_HEADS):
        state_ref[h] = states[h]


def hgrn2_mixer(rest3, hg_lb, hg_norm, layer):
    B, S, _ = rest3.shape
    depth = hg_lb.shape[0]
    C, W = HG_CHUNK, HG_HEAD_DIM
    rows = C * HG_STEP_CHUNKS

    def cols(base):
        return pl.BlockSpec((None, rows, HG_WIDTH), lambda b, c: (b, c, base * LANE // HG_WIDTH))

    windows = pltpu.VMEM((HG_STEP_CHUNKS * HG_HEADS, HG_PAD + C, W), _F32)
    return pl.pallas_call(
        functools.partial(_hg_kernel, layer),
        grid=(B, S // rows),
        in_specs=[cols(COL_ZF), cols(COL_HV), cols(COL_HQ), cols(COL_OG),
                  pl.BlockSpec((depth, 1, HG_WIDTH), lambda b, c: (0, 0, 0)),
                  pl.BlockSpec((None, 1, HG_WIDTH), lambda b, c: (layer, 0, 0))],
        out_specs=pl.BlockSpec((None, rows, HG_WIDTH), lambda b, c: (b, c, 0)),
        out_shape=jax.ShapeDtypeStruct((B, S, HG_WIDTH), _BF16),
        scratch_shapes=[pltpu.VMEM((HG_HEADS, W, W), _F32), windows, windows, windows],
        compiler_params=_params("arbitrary", "arbitrary"),
        name="hgrn2_mixer",
    )(rest3, rest3, rest3, rest3, hg_lb.reshape(depth, 1, HG_WIDTH), hg_norm.reshape(depth, 1, HG_WIDTH))


def _branch_mix_kernel(yp_ref, ys_ref, yh_ref, g0_ref, g1_ref, g2_ref, wp_ref, ws_ref, wh_ref, slab_ref,
                       o_ref, slab_bf_ref, wp_bf, ws_bf, wh_bf):
    @pl.when(pl.program_id(1) == 0)
    def _():
        wp_bf[...] = wp_ref[...].astype(_BF16)
        ws_bf[...] = ws_ref[...].astype(_BF16)
        wh_bf[...] = wh_ref[...].astype(_BF16)

    wp, ws, wh = wp_bf[...], ws_bf[...], wh_bf[...]
    for r0 in range(0, o_ref.shape[0], FFN_ROWS):
        rs = slice(r0, r0 + FFN_ROWS)

        def gated(g_ref, y_ref, w):
            return jax.nn.sigmoid(g_ref[rs, :].astype(_F32)) * _dot(y_ref[rs, :], w)

        mixed = gated(g0_ref, yp_ref, wp) + gated(g1_ref, ys_ref, ws) + gated(g2_ref, yh_ref, wh)
        o_ref[rs, :] = mixed.astype(o_ref.dtype)
    slab_bf_ref[...] = slab_ref[...].astype(_BF16)


def branch_mix(act, y_pool, y_sb, y_hg, w_pool, w_sb, w_hg, w_out, layer, tm=1024, tn=1024):
    T = act.shape[0]
    D = D_MODEL
    gate0 = QKV_WIDTH // tn
    per_gate = D // tn
    n_tok = T // tm
    slab_in, slab_out, slab_shape = _cast_slab_specs(w_out, layer, per_gate * n_tok, lambda n, i: n * n_tok + i)

    def gate(idx):
        return pl.BlockSpec((tm, tn), lambda n, i: (i, gate0 + idx * per_gate + n))

    def weight(width):
        return pl.BlockSpec((None, width, tn), lambda n, i: (layer, 0, n), pipeline_mode=pl.Buffered(1))

    return pl.pallas_call(
        _branch_mix_kernel,
        grid=(per_gate, n_tok),
        in_specs=[pl.BlockSpec((tm, POOL_WIDTH), lambda n, i: (i, 0)),
                  pl.BlockSpec((tm, SB_WIDTH), lambda n, i: (i, 0)),
                  pl.BlockSpec((tm, HG_WIDTH), lambda n, i: (i, 0)),
                  gate(0), gate(1), gate(2),
                  weight(POOL_WIDTH), weight(SB_WIDTH), weight(HG_WIDTH), slab_in],
        out_specs=[pl.BlockSpec((tm, tn), lambda n, i: (i, n)), slab_out],
        out_shape=[jax.ShapeDtypeStruct((T, D), _BF16), slab_shape],
        scratch_shapes=[pltpu.VMEM((POOL_WIDTH, tn), _BF16), pltpu.VMEM((SB_WIDTH, tn), _BF16),
                        pltpu.VMEM((HG_WIDTH, tn), _BF16)],
        compiler_params=_params("arbitrary", "arbitrary"),
        name="branch_mix",
    )(y_pool, y_sb, y_hg, act, act, act, w_pool, w_sb, w_hg, w_out)


def _out_proj_kernel(rows, x_ref, m_ref, w_ref, g_ref, o_ref, hn_ref):
    w = w_ref[...]
    for r0 in range(0, x_ref.shape[0], rows):
        rs = slice(r0, r0 + rows)
        out = x_ref[rs, :] + _dot(m_ref[rs, :], w)
        o_ref[rs, :] = out
        hn_ref[rs, :] = _rms_rows(out, g_ref[...]).astype(hn_ref.dtype)


def out_proj(x, mixed, w_out_bf, gains, layer, tm=512, rows=FFN_ROWS):
    T, D = x.shape
    return pl.pallas_call(
        functools.partial(_out_proj_kernel, rows),
        grid=(T // tm,),
        in_specs=[pl.BlockSpec((tm, D), lambda i: (i, 0)),
                  pl.BlockSpec((tm, D), lambda i: (i, 0)),
                  pl.BlockSpec((D, D), lambda i: (0, 0), pipeline_mode=pl.Buffered(1)),
                  pl.BlockSpec((None, 1, D), _layer_row(layer))],
        out_specs=[pl.BlockSpec((tm, D), lambda i: (i, 0)), pl.BlockSpec((tm, D), lambda i: (i, 0))],
        out_shape=[jax.ShapeDtypeStruct((T, D), _F32), jax.ShapeDtypeStruct((T, D), _BF16)],
        compiler_params=_params("arbitrary"),
        name="out_proj",
    )(x, mixed, w_out_bf, gains.reshape(-1, 1, D))


def _ffn_up_kernel(h_ref, wg_ref, wu_ref, slab_ref, o_ref, slab_bf_ref):
    wg = wg_ref[...].astype(_BF16)
    wu = wu_ref[...].astype(_BF16)
    for r0 in range(0, h_ref.shape[0], FFN_ROWS):
        rs = slice(r0, r0 + FFN_ROWS)
        h = h_ref[rs, :]
        o_ref[rs, :] = (_silu(_dot(h, wg)) * _dot(h, wu)).astype(o_ref.dtype)
    slab_bf_ref[...] = slab_ref[...].astype(_BF16)


def ffn_up(h, w_gate_up, w_down, layer, tm=2048, tn=512):
    T, D = h.shape
    ni, nj = T // tm, D_FF // tn
    slab_in, slab_out, slab_shape = _cast_slab_specs(w_down, layer, ni * nj, lambda i, j: i * nj + j)
    return pl.pallas_call(
        _ffn_up_kernel,
        grid=(ni, nj),
        in_specs=[pl.BlockSpec((tm, D), lambda i, j: (i, 0)),
                  pl.BlockSpec((None, D, tn), lambda i, j: (layer, 0, j)),
                  pl.BlockSpec((None, D, tn), lambda i, j: (layer, 0, nj + j)),
                  slab_in],
        out_specs=[pl.BlockSpec((tm, tn), lambda i, j: (i, j)), slab_out],
        out_shape=[jax.ShapeDtypeStruct((T, D_FF), _BF16), slab_shape],
        compiler_params=_params("arbitrary", "arbitrary"),
        name="ffn_up",
    )(h, w_gate_up, w_gate_up, w_down)


def _ffn_down_kernel(x_ref, a_ref, w_ref, slab_ref, o_ref, slab_bf_ref):
    o_ref[...] = x_ref[...] + _dot(a_ref[...], w_ref[...])
    slab_bf_ref[...] = slab_ref[...].astype(_BF16)


def ffn_down(x, hidden, w_down_bf, w_ple_gate, layer, tm=512):
    T, D = x.shape
    ni = T // tm
    slab_in, slab_out, slab_shape = _cast_slab_specs(w_ple_gate, layer, ni, lambda i: i)
    return pl.pallas_call(
        _ffn_down_kernel,
        grid=(ni,),
        in_specs=[pl.BlockSpec((tm, D), lambda i: (i, 0)),
                  pl.BlockSpec((tm, D_FF), lambda i: (i, 0)),
                  pl.BlockSpec((D_FF, D), lambda i: (0, 0), pipeline_mode=pl.Buffered(1)),
                  slab_in],
        out_specs=[pl.BlockSpec((tm, D), lambda i: (i, 0)), slab_out],
        out_shape=[jax.ShapeDtypeStruct((T, D), _F32), slab_shape],
        compiler_params=_params("arbitrary"),
        name="ffn_down",
    )(x, hidden, w_down_bf, w_ple_gate)


def _ple_kernel(finish, x_ref, g_ref, p_ref, wg_ref, wp_ref, gf_ref, o_ref):
    wg = wg_ref[...]
    wp = wp_ref[...].astype(_BF16)
    for r0 in range(0, x_ref.shape[0], FFN_ROWS):
        rs = slice(r0, r0 + FFN_ROWS)
        x = x_ref[rs, :]
        gate = jax.nn.sigmoid(_dot(_rms_rows(x, g_ref[...]).astype(_BF16), wg))
        out = x + gate * _dot(p_ref[rs, :].astype(_BF16), wp)
        o_ref[rs, :] = _rms_rows(out, gf_ref[...]) if finish else out


def ple_gate(x, gains, p, w_gate_bf, w_proj, final_gain, layer, finish, tm=512):
    T, D = x.shape
    return pl.pallas_call(
        functools.partial(_ple_kernel, finish),
        grid=(T // tm,),
        in_specs=[pl.BlockSpec((tm, D), lambda i: (i, 0)),
                  pl.BlockSpec((None, 1, D), _layer_row(layer)),
                  pl.BlockSpec((None, tm, PLE_DIM), lambda i: (layer, i, 0)),
                  pl.BlockSpec((D, D), lambda i: (0, 0), pipeline_mode=pl.Buffered(1)),
                  pl.BlockSpec((None, PLE_DIM, D), lambda i: (layer, 0, 0), pipeline_mode=pl.Buffered(1)),
                  pl.BlockSpec((1, D), lambda i: (0, 0))],
        out_specs=pl.BlockSpec((tm, D), lambda i: (i, 0)),
        out_shape=jax.ShapeDtypeStruct((T, D), _F32),
        compiler_params=_params("arbitrary"),
        name="ple_gate",
    )(x, gains.reshape(-1, 1, D), p.reshape(-1, T, PLE_DIM), w_gate_bf, w_proj, final_gain.reshape(1, D))


def kernel(x, p, norm_mix, w_in, pool_w, pool_scale, hg_lb, hg_norm, w_br_pool, w_br_sb, w_br_hg,
           w_out, norm_ffn, w_gate_up, w_down, norm_ple, w_ple_gate, w_ple_proj, norm_final):
    B, S, D = x.shape
    T = B * S
    depth = w_in.shape[0]
    xf = x.reshape(T, D)
    for i in range(depth):
        act, rest = norm_in_proj(xf, norm_mix, w_in, i)
        rest3 = rest.reshape(B, S, -1)
        y_pool = pool_mixer(rest3, pool_w[i].astype(_BF16), pool_scale[i])
        y_sb = stick_breaking(act.reshape(B, S, -1))
        y_hg = hgrn2_mixer(rest3, hg_lb, hg_norm, i)
        mixed, w_out_bf = branch_mix(act, y_pool.reshape(T, POOL_WIDTH), y_sb.reshape(T, SB_WIDTH),
                                     y_hg.reshape(T, HG_WIDTH), w_br_pool, w_br_sb, w_br_hg, w_out, i)
        xf, h_ffn = out_proj(xf, mixed, w_out_bf, norm_ffn, i)
        hidden, w_down_bf = ffn_up(h_ffn, w_gate_up, w_down, i)
        xf, w_ple_gate_bf = ffn_down(xf, hidden, w_down_bf, w_ple_gate, i)
        xf = ple_gate(xf, norm_ple, p, w_ple_gate_bf, w_ple_proj, norm_final, i, finish=(i == depth - 1))
    return xf.reshape(B, S, D)
```

```python
import functools

import jax
import jax.numpy as jnp
from jax import lax
from jax.experimental import pallas as pl
from jax.experimental.pallas import tpu as pltpu

D_MODEL = 2048
POOL_WIDTH = 512
POOL_WINDOWS = (2, 4, 8, 16)
POOL_GROUP_DIM = 128
SB_WIDTH = 1024
SB_HEAD_DIM = 128
SB_HEADS = 8
HG_WIDTH = 512
HG_HEAD_DIM = 128
HG_HEADS = 4
LB_FLOOR = 1e-20
D_FF = 5632
PLE_DIM = 256
EPS = 1e-6
IN_COLS = POOL_WIDTH + 3 * SB_WIDTH + 4 * HG_WIDTH + 3 * D_MODEL

LANE = 128

N_BRANCH = 3
QKV_WIDTH = 3 * SB_WIDTH
ACT_WIDTH = QKV_WIDTH + N_BRANCH * D_MODEL
REST_WIDTH = IN_COLS - ACT_WIDTH
COL_POOL = 0
COL_ZF = POOL_WIDTH // LANE
COL_HV = COL_ZF + HG_WIDTH // LANE
COL_HQ = COL_HV + HG_WIDTH // LANE
COL_OG = COL_HQ + HG_WIDTH // LANE

ATT_BLOCK = 128
ATT_GROUP = 3
ATT_HEADS = 8
ATT_QBLOCKS = 4
_LOG2E = 1.4426950408889634
ATT_LOG_ZERO = -105.0
HG_CHUNK = 128
HG_STEP_CHUNKS = 8
HG_BAND = 4
HG_PAD = 8
VMEM_LIMIT = 56 * 1024 * 1024
FFN_ROWS = 128
MIX_ROWS = 256
NORM_ROWS = 512
IN_PROJ_TILE = 512

_F32 = jnp.float32
_BF16 = jnp.bfloat16


def _params(*sem):
    return pltpu.CompilerParams(dimension_semantics=sem, vmem_limit_bytes=VMEM_LIMIT)


def _dot(a, b):
    return jnp.dot(a, b, preferred_element_type=_F32)


def _dot_nt(a, b):
    return lax.dot_general(a, b, (((1,), (1,)), ((), ())), preferred_element_type=_F32)


def _dot_tn(a, b):
    return lax.dot_general(a, b, (((0,), (0,)), ((), ())), preferred_element_type=_F32)


def _split_dot(lhs_fn, x):
    hi = x.astype(_BF16)
    lo = (x - hi.astype(_F32)).astype(_BF16)
    return lhs_fn(hi) + lhs_fn(lo)


def _rms_rows(x, g):
    return x * lax.rsqrt(jnp.mean(x * x, axis=-1, keepdims=True) + EPS) * g


def _silu(x):
    return x * jax.nn.sigmoid(x)


def _layer_row(layer):
    return lambda *_: (layer, 0, 0)


def _cast_slab_specs(w, layer, n_steps, step_of):
    rows = w.shape[1] // n_steps
    assert rows * n_steps == w.shape[1] and rows % 16 == 0
    in_spec = pl.BlockSpec((None, rows, w.shape[2]), lambda *idx: (layer, step_of(*idx), 0))
    out_spec = pl.BlockSpec((rows, w.shape[2]), lambda *idx: (step_of(*idx), 0))
    return in_spec, out_spec, jax.ShapeDtypeStruct(w.shape[1:], _BF16)


def _in_proj_kernel(x_hbm, g_ref, w_ref, act_ref, rest_ref, hn_ref, x_ref, x_sem):
    i, j = pl.program_id(0), pl.program_id(1)
    tm = x_ref.shape[0]

    def x_tile_copy(tile):
        return pltpu.make_async_copy(x_hbm.at[pl.ds(tile * tm, tm), :], x_ref, x_sem)

    def project(rows, h):
        r = _dot(h, w_ref[...].astype(_BF16))
        act_ref[rows, :] = r.astype(_BF16)
        rest_ref[rows, :] = r

    @pl.when(j == 0)
    def _():
        @pl.when(i == 0)
        def _():
            x_tile_copy(0).start()

        x_tile_copy(i).wait()
        for r0 in range(0, tm, NORM_ROWS):
            rs = slice(r0, r0 + NORM_ROWS)
            h = _rms_rows(x_ref[rs, :], g_ref[...]).astype(_BF16)
            hn_ref[rs, :] = h
            project(rs, h)

    @pl.when(j != 0)
    def _():
        @pl.when((j == 1) & (i + 1 < pl.num_programs(0)))
        def _():
            x_tile_copy(i + 1).start()

        project(slice(None), hn_ref[...])


def norm_in_proj(x, gains, w, layer, tm=2048, tn=IN_PROJ_TILE):
    T, D = x.shape
    n_pool, n_qkv, n_hg, n_gate = POOL_WIDTH // tn, QKV_WIDTH // tn, 4 * HG_WIDTH // tn, N_BRANCH * D // tn
    n_act, n_rest = n_qkv + n_gate, n_pool + n_hg

    def w_tile(i, j):
        tile = jnp.where(j < n_qkv, j + n_pool,
                         jnp.where(j < n_act, j + n_pool + n_hg,
                                   jnp.where(j < n_act + n_pool, j - n_act, j - n_gate)))
        return layer, 0, tile

    def act_tile(i, j):
        return i, jnp.minimum(j, n_act)

    def rest_tile(i, j):
        return i, jnp.where(j < n_act, n_rest, j - n_act)

    return pl.pallas_call(
        _in_proj_kernel,
        grid=(T // tm, n_act + n_rest),
        in_specs=[pl.BlockSpec(memory_space=pl.ANY),
                  pl.BlockSpec((None, 1, D), _layer_row(layer)),
                  pl.BlockSpec((None, D, tn), w_tile)],
        out_specs=[pl.BlockSpec((tm, tn), act_tile), pl.BlockSpec((tm, tn), rest_tile)],
        out_shape=[jax.ShapeDtypeStruct((T, ACT_WIDTH + tn), _BF16),
                   jax.ShapeDtypeStruct((T, REST_WIDTH + tn), _F32)],
        scratch_shapes=[pltpu.VMEM((tm, D), _BF16), pltpu.VMEM((tm, D), _F32), pltpu.SemaphoreType.DMA(())],
        compiler_params=_params("arbitrary", "arbitrary"),
        name="norm_in_proj",
    )(x, gains.reshape(-1, 1, D), w)


def _pool_kernel(u_ref, w_ref, s_ref, o_ref):
    S = u_ref.shape[0]
    t = lax.broadcasted_iota(jnp.int32, (S, POOL_GROUP_DIM), 0)
    for gi, win in enumerate(POOL_WINDOWS):
        cols = slice(gi * POOL_GROUP_DIM, (gi + 1) * POOL_GROUP_DIM)
        u = u_ref[:, cols]
        acc = u
        shift = 1
        while shift < win:
            acc = acc + jnp.where(t >= shift, pltpu.roll(acc, shift, axis=0), 0.0)
            shift *= 2
        cnt = jnp.minimum(t + 1, win).astype(_F32)
        mixed = (acc / cnt - u).astype(_BF16)
        y = _dot(mixed, w_ref[gi]) * s_ref[:, cols]
        o_ref[:, cols] = y.astype(o_ref.dtype)


def pool_mixer(proj3, pool_w, pool_scale):
    B, S, _ = proj3.shape
    return pl.pallas_call(
        _pool_kernel,
        grid=(B,),
        in_specs=[pl.BlockSpec((None, S, POOL_WIDTH), lambda b: (b, 0, COL_POOL)),
                  pl.BlockSpec((len(POOL_WINDOWS), POOL_GROUP_DIM, POOL_GROUP_DIM), lambda b: (0, 0, 0)),
                  pl.BlockSpec((1, POOL_WIDTH), lambda b: (0, 0))],
        out_specs=pl.BlockSpec((None, S, POOL_WIDTH), lambda b: (b, 0, 0)),
        out_shape=jax.ShapeDtypeStruct((B, S, POOL_WIDTH), _BF16),
        compiler_params=_params("arbitrary"),
        name="pool_mixer",
    )(proj3, pool_w, pool_scale.reshape(1, POOL_WIDTH))


def _sb_kernel(q_ref, k_ref, v_ref, o_ref):
    pair = pl.program_id(2)
    blk, group, heads, nq = ATT_BLOCK, ATT_GROUP, ATT_HEADS, ATT_QBLOCKS
    scale = SB_HEAD_DIM ** -0.5
    last_block = pair * nq + nq - 1
    q_all = q_ref[...]
    key_minus_query = (lax.broadcasted_iota(jnp.int32, (blk, blk), 1)
                       - lax.broadcasted_iota(jnp.int32, (blk, blk), 0))
    r2 = lax.broadcasted_iota(jnp.int32, (2 * blk, 2 * blk), 0) % blk
    c2 = lax.broadcasted_iota(jnp.int32, (2 * blk, 2 * blk), 1)
    neg_sums = jnp.where((r2 >= c2) | (c2 >= blk), -1.0, 0.0).astype(_BF16)

    def body(carry):
        step, _, accs, laters = carry
        diag_mask = key_minus_query < jnp.where(step == 0, 0, blk)
        parts = []
        for qb in range(nq):
            i = pair * nq + qb
            for hd in range(heads):
                cols = slice(hd * SB_HEAD_DIM, (hd + 1) * SB_HEAD_DIM)
                q = q_all[qb * blk:(qb + 1) * blk, cols]
                for u in range(group):
                    j = i - group * step - u
                    start = pl.multiple_of(jnp.maximum(j, 0) * blk, blk)
                    d = _dot_nt(q, k_ref[pl.ds(start, blk), cols])
                    z = d * scale
                    softplus = jnp.maximum(z, 0.0) + jnp.log(1.0 + jnp.exp2(jnp.abs(d) * (-scale * _LOG2E)))
                    if u == 0:
                        softplus = jnp.where(diag_mask, softplus, 0.0)
                    hi = softplus.astype(_BF16)
                    lo = (softplus - hi.astype(_F32)).astype(_BF16)
                    sums = _dot(jnp.concatenate([hi, lo], axis=1), neg_sums)
                    parts.append((qb * heads + hd, cols, u, j, start, z, sums))
        accs, laters = list(accs), list(laters)
        for slot, cols, u, j, start, z, sums in parts:
            a = jnp.exp(z + (sums[:, :blk] + laters[slot]))
            if u == 0:
                a = jnp.where(diag_mask, a, 0.0)
            vb = v_ref[pl.ds(start, blk), cols]
            vb = jnp.where(j >= 0, vb, jnp.zeros_like(vb))
            accs[slot] = accs[slot] + _dot(a.astype(_BF16), vb)
            laters[slot] = laters[slot] + sums[:, blk:]
        largest_later = jnp.max(functools.reduce(jnp.maximum, laters))
        return step + 1, largest_later, tuple(accs), tuple(laters)

    def unfinished(carry):
        step, largest_later, _, _ = carry
        return (step <= last_block // group) & (largest_later > ATT_LOG_ZERO)

    zeros = tuple(jnp.zeros((blk, blk), _F32) for _ in range(nq * heads))
    _, _, accs, _ = lax.while_loop(unfinished, body, (jnp.int32(0), jnp.float32(0.0), zeros, zeros))
    rows = [jnp.concatenate(accs[qb * heads:(qb + 1) * heads], axis=1) for qb in range(nq)]
    o_ref[...] = jnp.concatenate(rows, axis=0).astype(o_ref.dtype)


def stick_breaking(qkv3):
    B, S, _ = qkv3.shape
    heads = ATT_HEADS
    blk = ATT_BLOCK * ATT_QBLOCKS
    width = heads * SB_HEAD_DIM
    per = SB_WIDTH // width
    return pl.pallas_call(
        _sb_kernel,
        grid=(B, per, S // blk),
        in_specs=[pl.BlockSpec((None, blk, width), lambda b, h, i: (b, i, h)),
                  pl.BlockSpec((None, S, width), lambda b, h, i: (b, 0, per + h)),
                  pl.BlockSpec((None, S, width), lambda b, h, i: (b, 0, 2 * per + h))],
        out_specs=pl.BlockSpec((None, blk, width), lambda b, h, i: (b, i, h)),
        out_shape=jax.ShapeDtypeStruct((B, S, SB_WIDTH), _BF16),
        compiler_params=_params("arbitrary", "arbitrary", "arbitrary"),
        name="stick_breaking",
    )(qkv3, qkv3, qkv3)


def _hg_kernel(layer, zf_ref, hv_ref, hq_ref, og_ref, lb_ref, gn_ref, o_ref,
               state_ref, kpad_ref, bpad_ref, vpad_ref):
    C, W, pad = HG_CHUNK, HG_HEAD_DIM, HG_PAD
    heads = [slice(h * W, (h + 1) * W) for h in range(HG_HEADS)]

    @pl.when(pl.program_id(1) == 0)
    def _():
        state_ref[...] = jnp.zeros_like(state_ref)

    lbs = [lb_ref[d] for d in range(lb_ref.shape[0])]
    top = functools.reduce(jnp.maximum, lbs)
    es = [jnp.exp(row - top) for row in lbs]
    total = functools.reduce(jnp.add, es)
    sm = [e / total for e in es]
    lb = jnp.clip(functools.reduce(jnp.add, sm[:layer + 1]) - sm[0], 0.0, 1.0)

    r = lax.broadcasted_iota(jnp.int32, (C, C), 0)
    c = lax.broadcasted_iota(jnp.int32, (C, C), 1)
    tri = jnp.where(c <= r, 1.0, 0.0).astype(_BF16)
    sub = lax.broadcasted_iota(jnp.int32, (C, 1), 0) % HG_BAND
    zeros_pad = jnp.zeros((pad, W), _F32)

    def chunk(n, states):
        rows_n = slice(n * C, (n + 1) * C)
        z = zf_ref[rows_n, :]
        t = jnp.exp(-jnp.abs(z))
        big = 1.0 / (1.0 + t)
        small = t * big
        sig_pos = jnp.where(z >= 0.0, big, small)
        sig_neg = jnp.where(z >= 0.0, small, big)
        log_f = jnp.log(jnp.maximum(lb, LB_FLOOR) + (1.0 - jnp.minimum(lb, 1.0 - 1e-6)) * sig_pos)
        k = (1.0 - lb) * sig_neg
        q = _silu(hq_ref[rows_n, :])
        v = hv_ref[rows_n, :]
        v_bf = v.astype(_BF16)

        b = _split_dot(lambda part: _dot(tri, part), log_f) * _LOG2E

        q_dec = (q * jnp.exp2(b)).astype(_BF16)
        outs = [_dot_nt(q_dec[:, hs], states[h].astype(_BF16)) for h, hs in enumerate(heads)]

        scores = [jnp.zeros((C, C), _F32) for _ in heads]
        m = C // 2
        while m >= HG_BAND:
            span = 2 * m
            ref = jnp.concatenate(
                [jnp.broadcast_to(b[g * span + m - 1:g * span + m, :], (span, HG_WIDTH))
                 for g in range(C // span)], axis=0)
            qt = (q * jnp.exp2(jnp.minimum(b - ref, 0.0))).astype(_BF16)
            kt = (k * jnp.exp2(jnp.minimum(ref - b, 0.0))).astype(_BF16)
            sel = ((r // span) == (c // span)) & ((r % span) >= m) & ((c % span) < m)
            scores = [jnp.where(sel, _dot_nt(qt[:, hs], kt[:, hs]), scores[h]) for h, hs in enumerate(heads)]
            m //= 2
        outs = [outs[h] + _dot(scores[h].astype(_BF16), v_bf[:, hs]) for h, hs in enumerate(heads)]

        slots = [n * HG_HEADS + h for h in range(HG_HEADS)]
        for ref_, val in ((kpad_ref, k), (bpad_ref, b), (vpad_ref, v)):
            for h, hs in enumerate(heads):
                ref_[slots[h], 0:pad, :] = zeros_pad
                ref_[slots[h], pad:pad + C, :] = val[:, hs]
        for d in range(HG_BAND):
            rows = slice(pad - d, pad - d + C)
            for h, hs in enumerate(heads):
                decay = jnp.exp2(jnp.minimum(b[:, hs] - bpad_ref[slots[h], rows, :], 0.0))
                w = jnp.sum(q[:, hs] * kpad_ref[slots[h], rows, :] * decay, axis=-1, keepdims=True)
                outs[h] = outs[h] + jnp.where(sub >= d, w, 0.0) * vpad_ref[slots[h], rows, :]

        normed = [o * lax.rsqrt(jnp.mean(o * o, axis=-1, keepdims=True) + EPS) for o in outs]
        o_ref[rows_n, :] = (jnp.concatenate(normed, axis=1) * gn_ref[...]
                            * _silu(og_ref[rows_n, :])).astype(o_ref.dtype)

        b_last = b[C - 1:C, :]
        k_dec = (k * jnp.exp2(b_last - b)).astype(_BF16)
        keep = jnp.exp2(b_last)
        return [states[h] * keep[:, hs] + _dot_tn(v_bf[:, hs], k_dec[:, hs]) for h, hs in enumerate(heads)]

    states = [state_ref[h] for h in range(HG_HEADS)]
    for n in range(HG_STEP_CHUNKS):
        states = chunk(n, states)
    for h in range(HG_HEADS):
        state_ref[h] = states[h]


def hgrn2_mixer(rest3, hg_lb, hg_norm, layer):
    B, S, _ = rest3.shape
    depth = hg_lb.shape[0]
    C, W = HG_CHUNK, HG_HEAD_DIM
    rows = C * HG_STEP_CHUNKS

    def cols(base):
        return pl.BlockSpec((None, rows, HG_WIDTH), lambda b, c: (b, c, base * LANE // HG_WIDTH))

    windows = pltpu.VMEM((HG_STEP_CHUNKS * HG_HEADS, HG_PAD + C, W), _F32)
    return pl.pallas_call(
        functools.partial(_hg_kernel, layer),
        grid=(B, S // rows),
        in_specs=[cols(COL_ZF), cols(COL_HV), cols(COL_HQ), cols(COL_OG),
                  pl.BlockSpec((depth, 1, HG_WIDTH), lambda b, c: (0, 0, 0)),
                  pl.BlockSpec((None, 1, HG_WIDTH), lambda b, c: (layer, 0, 0))],
        out_specs=pl.BlockSpec((None, rows, HG_WIDTH), lambda b, c: (b, c, 0)),
        out_shape=jax.ShapeDtypeStruct((B, S, HG_WIDTH), _BF16),
        scratch_shapes=[pltpu.VMEM((HG_HEADS, W, W), _F32), windows, windows, windows],
        compiler_params=_params("arbitrary", "arbitrary"),
        name="hgrn2_mixer",
    )(rest3, rest3, rest3, rest3, hg_lb.reshape(depth, 1, HG_WIDTH), hg_norm.reshape(depth, 1, HG_WIDTH))


def _branch_mix_kernel(yp_ref, ys_ref, yh_ref, g0_ref, g1_ref, g2_ref, wp_ref, ws_ref, wh_ref, slab_ref,
                       o_ref, slab_bf_ref, wp_bf, ws_bf, wh_bf):
    @pl.when(pl.program_id(1) == 0)
    def _():
        wp_bf[...] = wp_ref[...].astype(_BF16)
        ws_bf[...] = ws_ref[...].astype(_BF16)
        wh_bf[...] = wh_ref[...].astype(_BF16)

    wp, ws, wh = wp_bf[...], ws_bf[...], wh_bf[...]
    for r0 in range(0, o_ref.shape[0], MIX_ROWS):
        rs = slice(r0, r0 + MIX_ROWS)

        def gated(g_ref, y_ref, w):
            return jax.nn.sigmoid(g_ref[rs, :].astype(_F32)) * _dot(y_ref[rs, :], w)

        mixed = gated(g0_ref, yp_ref, wp) + gated(g1_ref, ys_ref, ws) + gated(g2_ref, yh_ref, wh)
        o_ref[rs, :] = mixed.astype(o_ref.dtype)
    slab_bf_ref[...] = slab_ref[...].astype(_BF16)


def branch_mix(act, y_pool, y_sb, y_hg, w_pool, w_sb, w_hg, w_out, layer, tm=1024, tn=1024):
    T = act.shape[0]
    D = D_MODEL
    gate0 = QKV_WIDTH // tn
    per_gate = D // tn
    n_tok = T // tm
    slab_in, slab_out, slab_shape = _cast_slab_specs(w_out, layer, per_gate * n_tok, lambda n, i: n * n_tok + i)

    def gate(idx):
        return pl.BlockSpec((tm, tn), lambda n, i: (i, gate0 + idx * per_gate + n))

    def weight(width):
        return pl.BlockSpec((None, width, tn), lambda n, i: (layer, 0, n), pipeline_mode=pl.Buffered(1))

    return pl.pallas_call(
        _branch_mix_kernel,
        grid=(per_gate, n_tok),
        in_specs=[pl.BlockSpec((tm, POOL_WIDTH), lambda n, i: (i, 0)),
                  pl.BlockSpec((tm, SB_WIDTH), lambda n, i: (i, 0)),
                  pl.BlockSpec((tm, HG_WIDTH), lambda n, i: (i, 0)),
                  gate(0), gate(1), gate(2),
                  weight(POOL_WIDTH), weight(SB_WIDTH), weight(HG_WIDTH), slab_in],
        out_specs=[pl.BlockSpec((tm, tn), lambda n, i: (i, n)), slab_out],
        out_shape=[jax.ShapeDtypeStruct((T, D), _BF16), slab_shape],
        scratch_shapes=[pltpu.VMEM((POOL_WIDTH, tn), _BF16), pltpu.VMEM((SB_WIDTH, tn), _BF16),
                        pltpu.VMEM((HG_WIDTH, tn), _BF16)],
        compiler_params=_params("arbitrary", "arbitrary"),
        name="branch_mix",
    )(y_pool, y_sb, y_hg, act, act, act, w_pool, w_sb, w_hg, w_out)


def _out_proj_kernel(rows, x_ref, m_ref, w_ref, g_ref, o_ref, hn_ref):
    w = w_ref[...]
    for r0 in range(0, x_ref.shape[0], rows):
        rs = slice(r0, r0 + rows)
        out = x_ref[rs, :] + _dot(m_ref[rs, :], w)
        o_ref[rs, :] = out
        hn_ref[rs, :] = _rms_rows(out, g_ref[...]).astype(hn_ref.dtype)


def out_proj(x, mixed, w_out_bf, gains, layer, tm=512, rows=FFN_ROWS):
    T, D = x.shape
    return pl.pallas_call(
        functools.partial(_out_proj_kernel, rows),
        grid=(T // tm,),
        in_specs=[pl.BlockSpec((tm, D), lambda i: (i, 0)),
                  pl.BlockSpec((tm, D), lambda i: (i, 0)),
                  pl.BlockSpec((D, D), lambda i: (0, 0), pipeline_mode=pl.Buffered(1)),
                  pl.BlockSpec((None, 1, D), _layer_row(layer))],
        out_specs=[pl.BlockSpec((tm, D), lambda i: (i, 0)), pl.BlockSpec((tm, D), lambda i: (i, 0))],
        out_shape=[jax.ShapeDtypeStruct((T, D), _F32), jax.ShapeDtypeStruct((T, D), _BF16)],
        compiler_params=_params("arbitrary"),
        name="out_proj",
    )(x, mixed, w_out_bf, gains.reshape(-1, 1, D))


def _ffn_up_kernel(h_ref, wg_ref, wu_ref, slab_ref, o_ref, slab_bf_ref):
    wg = wg_ref[...].astype(_BF16)
    wu = wu_ref[...].astype(_BF16)
    for r0 in range(0, h_ref.shape[0], FFN_ROWS):
        rs = slice(r0, r0 + FFN_ROWS)
        h = h_ref[rs, :]
        o_ref[rs, :] = (_silu(_dot(h, wg)) * _dot(h, wu)).astype(o_ref.dtype)
    slab_bf_ref[...] = slab_ref[...].astype(_BF16)


def ffn_up(h, w_gate_up, w_down, layer, tm=2048, tn=512):
    T, D = h.shape
    ni, nj = T // tm, D_FF // tn
    slab_in, slab_out, slab_shape = _cast_slab_specs(w_down, layer, ni * nj, lambda i, j: i * nj + j)
    return pl.pallas_call(
        _ffn_up_kernel,
        grid=(ni, nj),
        in_specs=[pl.BlockSpec((tm, D), lambda i, j: (i, 0)),
                  pl.BlockSpec((None, D, tn), lambda i, j: (layer, 0, j)),
                  pl.BlockSpec((None, D, tn), lambda i, j: (layer, 0, nj + j)),
                  slab_in],
        out_specs=[pl.BlockSpec((tm, tn), lambda i, j: (i, j)), slab_out],
        out_shape=[jax.ShapeDtypeStruct((T, D_FF), _BF16), slab_shape],
        compiler_params=_params("arbitrary", "arbitrary"),
        name="ffn_up",
    )(h, w_gate_up, w_gate_up, w_down)


def _ffn_down_kernel(x_ref, a_ref, w_ref, slab_ref, o_ref, slab_bf_ref):
    o_ref[...] = x_ref[...] + _dot(a_ref[...], w_ref[...])
    slab_bf_ref[...] = slab_ref[...].astype(_BF16)


def ffn_down(x, hidden, w_down_bf, w_ple_gate, layer, tm=512):
    T, D = x.shape
    ni = T // tm
    slab_in, slab_out, slab_shape = _cast_slab_specs(w_ple_gate, layer, ni, lambda i: i)
    return pl.pallas_call(
        _ffn_down_kernel,
        grid=(ni,),
        in_specs=[pl.BlockSpec((tm, D), lambda i: (i, 0)),
                  pl.BlockSpec((tm, D_FF), lambda i: (i, 0)),
                  pl.BlockSpec((D_FF, D), lambda i: (0, 0), pipeline_mode=pl.Buffered(1)),
                  slab_in],
        out_specs=[pl.BlockSpec((tm, D), lambda i: (i, 0)), slab_out],
        out_shape=[jax.ShapeDtypeStruct((T, D), _F32), slab_shape],
        compiler_params=_params("arbitrary"),
        name="ffn_down",
    )(x, hidden, w_down_bf, w_ple_gate)


def _ple_kernel(finish, x_ref, g_ref, p_ref, wg_ref, wp_ref, gf_ref, o_ref):
    wg = wg_ref[...]
    wp = wp_ref[...].astype(_BF16)
    for r0 in range(0, x_ref.shape[0], FFN_ROWS):
        rs = slice(r0, r0 + FFN_ROWS)
        x = x_ref[rs, :]
        gate = jax.nn.sigmoid(_dot(_rms_rows(x, g_ref[...]).astype(_BF16), wg))
        out = x + gate * _dot(p_ref[rs, :].astype(_BF16), wp)
        o_ref[rs, :] = _rms_rows(out, gf_ref[...]) if finish else out


def ple_gate(x, gains, p, w_gate_bf, w_proj, final_gain, layer, finish, tm=512):
    T, D = x.shape
    return pl.pallas_call(
        functools.partial(_ple_kernel, finish),
        grid=(T // tm,),
        in_specs=[pl.BlockSpec((tm, D), lambda i: (i, 0)),
                  pl.BlockSpec((None, 1, D), _layer_row(layer)),
                  pl.BlockSpec((None, tm, PLE_DIM), lambda i: (layer, i, 0)),
                  pl.BlockSpec((D, D), lambda i: (0, 0), pipeline_mode=pl.Buffered(1)),
                  pl.BlockSpec((None, PLE_DIM, D), lambda i: (layer, 0, 0), pipeline_mode=pl.Buffered(1)),
                  pl.BlockSpec((1, D), lambda i: (0, 0))],
        out_specs=pl.BlockSpec((tm, D), lambda i: (i, 0)),
        out_shape=jax.ShapeDtypeStruct((T, D), _F32),
        compiler_params=_params("arbitrary"),
        name="ple_gate",
    )(x, gains.reshape(-1, 1, D), p.reshape(-1, T, PLE_DIM), w_gate_bf, w_proj, final_gain.reshape(1, D))


def kernel(x, p, norm_mix, w_in, pool_w, pool_scale, hg_lb, hg_norm, w_br_pool, w_br_sb, w_br_hg,
           w_out, norm_ffn, w_gate_up, w_down, norm_ple, w_ple_gate, w_ple_proj, norm_final):
    B, S, D = x.shape
    T = B * S
    depth = w_in.shape[0]
    xf = x.reshape(T, D)
    for i in range(depth):
        act, rest = norm_in_proj(xf, norm_mix, w_in, i)
        rest3 = rest.reshape(B, S, -1)
        y_pool = pool_mixer(rest3, pool_w[i].astype(_BF16), pool_scale[i])
        y_sb = stick_breaking(act.reshape(B, S, -1))
        y_hg = hgrn2_mixer(rest3, hg_lb, hg_norm, i)
        mixed, w_out_bf = branch_mix(act, y_pool.reshape(T, POOL_WIDTH), y_sb.reshape(T, SB_WIDTH),
                                     y_hg.reshape(T, HG_WIDTH), w_br_pool, w_br_sb, w_br_hg, w_out, i)
        xf, h_ffn = out_proj(xf, mixed, w_out_bf, norm_ffn, i)
        hidden, w_down_bf = ffn_up(h_ffn, w_gate_up, w_down, i)
        xf, w_ple_gate_bf = ffn_down(xf, hidden, w_down_bf, w_ple_gate, i)
        xf = ple_gate(xf, norm_ple, p, w_ple_gate_bf, w_ple_proj, norm_final, i, finish=(i == depth - 1))
    return xf.reshape(B, S, D)
```
